```python
import jax, jax.numpy as jnp
from jax import lax
import numpy as np

D_MODEL = 2048
BATCH = 2
SEQ = 8192
DEPTH = 2

CHUNK = 64
Q_BLOCK = 128
N_MEM = 256
D_FF = 5632
CONV_CH = D_MODEL // 2
CONV_WIDTH = 31
MLA_HEADS = 8
QK_NOPE = 128
QK_ROPE = 64
QK_HEAD = QK_NOPE + QK_ROPE
V_HEAD = 128
Q_LORA = 768
KV_LORA = 256
ROPE_THETA = 10000.0
X_HEADS = 4
X_HEAD_DIM = D_MODEL // X_HEADS
IN_COLS = 2 * CONV_CH + Q_LORA + KV_LORA + QK_ROPE
MIX_OUT = CONV_CH + MLA_HEADS * V_HEAD
EPS = 1e-6
NEG_INF = -1e30

kernel_name = "hybrid_conv_mla_macaron_encoder"


def rms_norm(x, g):
    xf = x.astype(jnp.float32)
    y = xf * lax.rsqrt(jnp.mean(xf * xf, axis=-1, keepdims=True) + EPS)
    return (y * g.astype(jnp.float32)).astype(x.dtype)


def layer_norm(x, g, b):
    xf = x.astype(jnp.float32)
    mu = jnp.mean(xf, axis=-1, keepdims=True)
    var = jnp.mean(jnp.square(xf - mu), axis=-1, keepdims=True)
    y = (xf - mu) * lax.rsqrt(var + EPS)
    return (y * g.astype(jnp.float32) + b.astype(jnp.float32)).astype(x.dtype)


def swiglu_ffn(x, norm_g, w_gate, w_up, w_down):
    h = rms_norm(x, norm_g)
    return (jax.nn.silu(h @ w_gate) * (h @ w_up)) @ w_down


def rope_tables(positions):
    inv_freq = 1.0 / (ROPE_THETA ** (jnp.arange(0, QK_ROPE, 2, dtype=jnp.float32) / QK_ROPE))
    ang = positions.astype(jnp.float32)[..., None] * inv_freq
    return jnp.cos(ang)[:, :, None, :], jnp.sin(ang)[:, :, None, :]


def apply_rope(x, cos, sin):
    c = cos.astype(x.dtype)
    s = sin.astype(x.dtype)
    x1, x2 = x[..., : QK_ROPE // 2], x[..., QK_ROPE // 2:]
    return jnp.concatenate([x1 * c - x2 * s, x2 * c + x1 * s], axis=-1)


def conv_module(u, conv_w, conv_b, ln_g, ln_b):
    a = u[..., :CONV_CH] * jax.nn.sigmoid(u[..., CONV_CH:])
    y = lax.conv_general_dilated(
        a, conv_w[:, None, :], window_strides=(1,),
        padding=[(CONV_WIDTH - 1, 0)],
        dimension_numbers=("NWC", "WIO", "NWC"),
        feature_group_count=CONV_CH) + conv_b
    return jax.nn.silu(layer_norm(y, ln_g, ln_b))


def chunk_causal_attention(q, k, v):
    B, S, H, Dq = q.shape
    Dv = v.shape[-1]
    nb = S // Q_BLOCK
    scale = Dq ** -0.5
    key_chunk = jnp.arange(S) // CHUNK
    qb = q.reshape(B, nb, Q_BLOCK, H, Dq).transpose(1, 0, 2, 3, 4)

    def one_block(args):
        qi, i = args
        s = jnp.einsum("bqhd,bkhd->bhqk", qi, k).astype(jnp.float32) * scale
        q_chunk = (i * Q_BLOCK + jnp.arange(Q_BLOCK)) // CHUNK
        mask = key_chunk[None, :] <= q_chunk[:, None]
        s = jnp.where(mask[None, None], s, NEG_INF)
        p = jax.nn.softmax(s, axis=-1).astype(v.dtype)
        return jnp.einsum("bhqk,bkhd->bqhd", p, v)

    out = lax.map(one_block, (qb, jnp.arange(nb)))
    return out.transpose(1, 0, 2, 3, 4).reshape(B, S, H, Dv)


def mla_group(c_q, c_kv, k_pe, cos, sin, q_a_norm, w_q_b, kv_a_norm, w_kv_b, q_norm, k_norm):
    B, S, _ = c_q.shape
    q = (rms_norm(c_q, q_a_norm) @ w_q_b).reshape(B, S, MLA_HEADS, QK_HEAD)
    kv = (rms_norm(c_kv, kv_a_norm) @ w_kv_b).reshape(B, S, MLA_HEADS, QK_NOPE + V_HEAD)
    k_nope, v = kv[..., :QK_NOPE], kv[..., QK_NOPE:]
    k = jnp.concatenate(
        [k_nope, jnp.broadcast_to(k_pe[:, :, None, :], (B, S, MLA_HEADS, QK_ROPE))], axis=-1)
    q = rms_norm(q, q_norm)
    k = rms_norm(k, k_norm)
    q = jnp.concatenate([q[..., :QK_NOPE], apply_rope(q[..., QK_NOPE:], cos, sin)], axis=-1)
    k = jnp.concatenate([k[..., :QK_NOPE], apply_rope(k[..., QK_NOPE:], cos, sin)], axis=-1)
    o = chunk_causal_attention(q, k, v)
    return o.reshape(B, S, MLA_HEADS * V_HEAD)


def memory_cross_attention(x, mem, cross_norm, mem_norm, w_cq, w_ck, w_cv, cq_norm, ck_norm, w_co):
    B, S, D = x.shape
    M = mem.shape[1]
    h = rms_norm(x, cross_norm)
    m = rms_norm(mem, mem_norm)
    q = rms_norm((h @ w_cq).reshape(B, S, X_HEADS, X_HEAD_DIM), cq_norm)
    k = rms_norm((m @ w_ck).reshape(B, M, X_HEADS, X_HEAD_DIM), ck_norm)
    v = (m @ w_cv).reshape(B, M, X_HEADS, X_HEAD_DIM)
    s = jnp.einsum("bqhd,bmhd->bhqm", q, k).astype(jnp.float32) * (X_HEAD_DIM ** -0.5)
    p = jax.nn.softmax(s, axis=-1).astype(v.dtype)
    o = jnp.einsum("bhqm,bmhd->bqhd", p, v).reshape(B, S, D)
    return o @ w_co


def setup_inputs(seed: int = 0) -> dict:
    key = jax.random.key(seed)
    ks = iter(jax.random.split(key, 64))
    L = DEPTH

    def nrm(shape):
        return jax.random.normal(next(ks), shape, jnp.float32)

    def w(shape, fan_in):
        return nrm(shape) * (fan_in ** -0.5)

    def gain(shape):
        return 1.0 + 0.02 * nrm(shape)

    def bias(shape):
        return 0.02 * nrm(shape)

    x = nrm((BATCH, SEQ, D_MODEL))
    mem = nrm((BATCH, N_MEM, D_MODEL))
    offs = jax.random.randint(next(ks), (BATCH,), 0, 1024) * CHUNK
    positions = (offs[:, None] + jnp.arange(SEQ)[None, :]).astype(jnp.int32)
    return {
        "x": x,
        "mem": mem,
        "positions": positions,
        "ffn1_norm": gain((L, D_MODEL)),
        "ffn1_w_gate": w((L, D_MODEL, D_FF), D_MODEL),
        "ffn1_w_up": w((L, D_MODEL, D_FF), D_MODEL),
        "ffn1_w_down": w((L, D_FF, D_MODEL), D_FF),
        "mix_norm": gain((L, D_MODEL)),
        "w_in": w((L, D_MODEL, IN_COLS), D_MODEL),
        "conv_w": w((L, CONV_WIDTH, CONV_CH), CONV_WIDTH),
        "conv_b": bias((L, CONV_CH)),
        "conv_ln_g": gain((L, CONV_CH)),
        "conv_ln_b": bias((L, CONV_CH)),
        "q_a_norm": gain((L, Q_LORA)),
        "w_q_b": w((L, Q_LORA, MLA_HEADS * QK_HEAD), Q_LORA),
        "kv_a_norm": gain((L, KV_LORA)),
        "w_kv_b": w((L, KV_LORA, MLA_HEADS * (QK_NOPE + V_HEAD)), KV_LORA),
        "q_norm": gain((L, QK_HEAD)),
        "k_norm": gain((L, QK_HEAD)),
        "w_out": w((L, MIX_OUT, D_MODEL), MIX_OUT),
        "cross_norm": gain((L, D_MODEL)),
        "mem_norm": gain((L, D_MODEL)),
        "w_cq": w((L, D_MODEL, D_MODEL), D_MODEL),
        "w_ck": w((L, D_MODEL, D_MODEL), D_MODEL),
        "w_cv": w((L, D_MODEL, D_MODEL), D_MODEL),
        "cq_norm": gain((L, X_HEAD_DIM)),
        "ck_norm": gain((L, X_HEAD_DIM)),
        "w_co": w((L, D_MODEL, D_MODEL), D_MODEL),
        "ffn2_norm": gain((L, D_MODEL)),
        "ffn2_w_gate": w((L, D_MODEL, D_FF), D_MODEL),
        "ffn2_w_up": w((L, D_MODEL, D_FF), D_MODEL),
        "ffn2_w_down": w((L, D_FF, D_MODEL), D_FF),
    }


def reference(x, mem, positions,
              ffn1_norm, ffn1_w_gate, ffn1_w_up, ffn1_w_down,
              mix_norm, w_in, conv_w, conv_b, conv_ln_g, conv_ln_b,
              q_a_norm, w_q_b, kv_a_norm, w_kv_b, q_norm, k_norm, w_out,
              cross_norm, mem_norm, w_cq, w_ck, w_cv, cq_norm, ck_norm, w_co,
              ffn2_norm, ffn2_w_gate, ffn2_w_up, ffn2_w_down):
    cos, sin = rope_tables(positions)
    o_q = 2 * CONV_CH
    o_kv = o_q + Q_LORA
    o_pe = o_kv + KV_LORA
    for l in range(DEPTH):
        x = x + 0.5 * swiglu_ffn(x, ffn1_norm[l], ffn1_w_gate[l], ffn1_w_up[l], ffn1_w_down[l])

        h = rms_norm(x, mix_norm[l])
        z = h @ w_in[l]
        y_conv = conv_module(z[..., :o_q], conv_w[l], conv_b[l], conv_ln_g[l], conv_ln_b[l])
        y_mla = mla_group(z[..., o_q:o_kv], z[..., o_kv:o_pe], z[..., o_pe:], cos, sin,
                          q_a_norm[l], w_q_b[l], kv_a_norm[l], w_kv_b[l], q_norm[l], k_norm[l])
        x = x + jnp.concatenate([y_conv, y_mla], axis=-1) @ w_out[l]

        x = x + memory_cross_attention(x, mem, cross_norm[l], mem_norm[l], w_cq[l], w_ck[l],
                                       w_cv[l], cq_norm[l], ck_norm[l], w_co[l])

        x = x + 0.5 * swiglu_ffn(x, ffn2_norm[l], ffn2_w_gate[l], ffn2_w_up[l], ffn2_w_down[l])
    return x
```

```python
import functools

import jax
import jax.numpy as jnp
from jax import lax
from jax.experimental import pallas as pl
from jax.experimental.pallas import tpu as pltpu

CHUNK = 64
CONV_WIDTH = 31
MLA_HEADS = 8
QK_NOPE = 128
QK_ROPE = 64
QK_HEAD = QK_NOPE + QK_ROPE
V_HEAD = 128
Q_LORA = 768
KV_LORA = 256
ROPE_THETA = 10000.0
X_HEADS = 4
EPS = 1e-6
NEG_INF = -1e30

LANES = 128
SUBLANES = 8
VMEM_BYTES = 64 * 1024 * 1024
HEAD_PAD = 2 * LANES

F32 = jnp.float32
BF16 = jnp.bfloat16


def _params(vmem_mib):
    assert vmem_mib * 1024 * 1024 <= VMEM_BYTES
    return pltpu.CompilerParams(vmem_limit_bytes=vmem_mib * 1024 * 1024)


def _rms(x, g):
    ms = jnp.mean(x * x, axis=-1, keepdims=True)
    return x * lax.rsqrt(ms + EPS) * g


def _dot(a, b):
    return jnp.dot(a, b, preferred_element_type=F32)


def _dot_nt(a, b):
    return lax.dot_general(a, b, (((1,), (1,)), ((), ())), preferred_element_type=F32)


def _ffn_body(x_ref, g_ref, wg_ref, wu_ref, wd_ref, o_ref, h_ref, acc_ref):
    j = pl.program_id(1)

    @pl.when(j == 0)
    def _():
        h_ref[...] = _rms(x_ref[...], g_ref[...]).astype(BF16)
        acc_ref[...] = jnp.zeros_like(acc_ref)

    h = h_ref[...]
    gate = _dot(h, wg_ref[...])
    up = _dot(h, wu_ref[...])
    mid = (gate * jax.nn.sigmoid(gate) * up).astype(BF16)
    acc_ref[...] += _dot(mid, wd_ref[...])

    @pl.when(j == pl.num_programs(1) - 1)
    def _():
        o_ref[...] = x_ref[...] + 0.5 * acc_ref[...]


def _ffn(x, g, wg, wu, wd, *, tm=512, tf=512):
    t, d = x.shape
    f = wg.shape[1]
    tm, tf = min(tm, t), min(tf, f)
    return pl.pallas_call(
        _ffn_body,
        grid=(t // tm, f // tf),
        in_specs=[
            pl.BlockSpec((tm, d), lambda i, j: (i, 0)),
            pl.BlockSpec((1, d), lambda i, j: (0, 0)),
            pl.BlockSpec((d, tf), lambda i, j: (0, j)),
            pl.BlockSpec((d, tf), lambda i, j: (0, j)),
            pl.BlockSpec((tf, d), lambda i, j: (j, 0)),
        ],
        out_specs=pl.BlockSpec((tm, d), lambda i, j: (i, 0)),
        out_shape=jax.ShapeDtypeStruct((t, d), F32),
        scratch_shapes=[pltpu.VMEM((tm, d), BF16), pltpu.VMEM((tm, d), F32)],
        compiler_params=_params(48),
        name="ffn",
    )(x, g, wg, wu, wd)


def _rope_tab_body(pos_ref, freq_ref, mc_ref, m1_ref, m2_ref, c_ref, s1_ref, s2_ref):
    ang = pos_ref[...].astype(F32) * freq_ref[...]
    cos, sin = jnp.cos(ang), jnp.sin(ang)
    c_ref[...] = cos * mc_ref[...]
    s1_ref[...] = sin * m1_ref[...]
    s2_ref[...] = sin * m2_ref[...]


def _rope_tables(positions, *, ts=1024):
    b, s = positions.shape
    ts = min(ts, s)
    half = QK_ROPE // 2
    inv_freq = 1.0 / (ROPE_THETA ** (jnp.arange(0, QK_ROPE, 2, dtype=F32) / QK_ROPE))
    zeros = jnp.zeros((LANES - QK_ROPE,), F32)
    ones, zhalf = jnp.ones((half,), F32), jnp.zeros((half,), F32)
    freq = jnp.concatenate([inv_freq, inv_freq, zeros])[None]
    mc = jnp.concatenate([ones, ones, zeros])[None]
    m1 = jnp.concatenate([-ones, zhalf, zeros])[None]
    m2 = jnp.concatenate([zhalf, ones, zeros])[None]
    row = pl.BlockSpec((1, LANES), lambda bi, i: (0, 0))
    tab = pl.BlockSpec((None, ts, LANES), lambda bi, i: (bi, i, 0))
    shape = jax.ShapeDtypeStruct((b, s, LANES), F32)
    return pl.pallas_call(
        _rope_tab_body,
        grid=(b, s // ts),
        in_specs=[pl.BlockSpec((None, ts, 1), lambda bi, i: (bi, i, 0)), row, row, row, row],
        out_specs=[tab, tab, tab],
        out_shape=[shape, shape, shape],
        name="rope_tables",
    )(positions[..., None], freq, mc, m1, m2)


def _rope(r, c, s1, s2):
    return r * c + pltpu.roll(r, LANES - QK_ROPE // 2, 1) * s1 + pltpu.roll(r, QK_ROPE // 2, 1) * s2


def _inproj_body(x_ref, g_ref, wc_ref, wq_ref, wkv_ref, wpe_ref, gq_ref, gkv_ref,
                 a_ref, cq_ref, ckv_ref, kpe_ref):
    h = _rms(x_ref[...], g_ref[...]).astype(BF16)
    cc = a_ref.shape[-1]
    a_ref[...] = _dot(h, wc_ref[:, :cc]) * jax.nn.sigmoid(_dot(h, wc_ref[:, cc:]))
    cq_ref[...] = _rms(_dot(h, wq_ref[...]), gq_ref[...]).astype(BF16)
    ckv_ref[...] = _rms(_dot(h, wkv_ref[...]), gkv_ref[...]).astype(BF16)
    kpe_ref[...] = _dot(h, wpe_ref[...])


def _inproj(x, g, wc, wq, wkv, wpe, gq, gkv, *, tm=512):
    b, s, d = x.shape
    tm = min(tm, s)
    cc = wc.shape[1] // 2

    def const(shape):
        return pl.BlockSpec(shape, lambda bi, i: (0, 0))

    def tile(c):
        return pl.BlockSpec((None, tm, c), lambda bi, i: (bi, i, 0))

    return pl.pallas_call(
        _inproj_body,
        grid=(b, s // tm),
        in_specs=[tile(d), const((1, d)), const(wc.shape), const(wq.shape), const(wkv.shape),
                  const(wpe.shape), const((1, Q_LORA)), const((1, KV_LORA))],
        out_specs=[tile(cc), tile(Q_LORA), tile(KV_LORA), tile(LANES)],
        out_shape=[jax.ShapeDtypeStruct((b, s, cc), F32),
                   jax.ShapeDtypeStruct((b, s, Q_LORA), BF16),
                   jax.ShapeDtypeStruct((b, s, KV_LORA), BF16),
                   jax.ShapeDtypeStruct((b, s, LANES), F32)],
        compiler_params=_params(56),
        name="inproj",
    )(x, g, wc, wq, wkv, wpe, gq, gkv)


def _qkv_body(cq_ref, ckv_ref, kpe_ref, c_ref, s1_ref, s2_ref, wq_ref, wk_ref, wv_ref,
              gq_ref, gkn_ref, gkp_ref, q_ref, k_ref, v_ref):
    c, s1, s2 = c_ref[...], s1_ref[...], s2_ref[...]
    scale = QK_HEAD ** -0.5
    cq, ckv = cq_ref[...], ckv_ref[...]

    kpe = kpe_ref[...]
    kpe_ss = jnp.sum(kpe * kpe, axis=-1, keepdims=True)
    kpe_rot = _rope(kpe * gkp_ref[...], c, s1, s2)

    for hd in range(MLA_HEADS):
        qh = _dot(cq, wq_ref[:, hd * HEAD_PAD:(hd + 1) * HEAD_PAD])
        ss = jnp.sum(qh * qh, axis=-1, keepdims=True)
        qn = qh * (lax.rsqrt(ss * (1.0 / QK_HEAD) + EPS) * scale) * gq_ref[...]
        q_ref[hd, :, :LANES] = qn[:, :LANES].astype(BF16)
        q_ref[hd, :, LANES:] = _rope(qn[:, LANES:], c, s1, s2).astype(BF16)

        kn = _dot(ckv, wk_ref[:, hd * QK_NOPE:(hd + 1) * QK_NOPE])
        ss = jnp.sum(kn * kn, axis=-1, keepdims=True) + kpe_ss
        rinv = lax.rsqrt(ss * (1.0 / QK_HEAD) + EPS)
        k_ref[hd, :, :LANES] = (kn * rinv * gkn_ref[...]).astype(BF16)
        k_ref[hd, :, LANES:] = (kpe_rot * rinv).astype(BF16)

        v_ref[hd] = _dot(ckv, wv_ref[:, hd * V_HEAD:(hd + 1) * V_HEAD]).astype(BF16)


def _qkv(cq, ckv, kpe, tabs, wq, wk, wv, gq, gkn, gkp, *, tm=512):
    b, s, _ = cq.shape
    tm = min(tm, s)

    def const(shape):
        return pl.BlockSpec(shape, lambda bi, i: (0, 0))

    def tile(c):
        return pl.BlockSpec((None, tm, c), lambda bi, i: (bi, i, 0))

    def heads(c):
        return pl.BlockSpec((None, MLA_HEADS, tm, c), lambda bi, i: (bi, 0, i, 0))

    return pl.pallas_call(
        _qkv_body,
        grid=(b, s // tm),
        in_specs=[tile(Q_LORA), tile(KV_LORA), tile(LANES), tile(LANES), tile(LANES), tile(LANES),
                  const(wq.shape), const(wk.shape), const(wv.shape),
                  const((1, HEAD_PAD)), const((1, LANES)), const((1, LANES))],
        out_specs=[heads(HEAD_PAD), heads(HEAD_PAD), heads(V_HEAD)],
        out_shape=[jax.ShapeDtypeStruct((b, MLA_HEADS, s, HEAD_PAD), BF16),
                   jax.ShapeDtypeStruct((b, MLA_HEADS, s, HEAD_PAD), BF16),
                   jax.ShapeDtypeStruct((b, MLA_HEADS, s, V_HEAD), BF16)],
        compiler_params=_params(40),
        name="qkv",
    )(cq, ckv, kpe, *tabs, wq, wk, wv, gq, gkn, gkp)


def _attn_body(q_ref, k_ref, v_ref, o_ref, *, blk):
    qi = pl.program_id(2)
    q = q_ref[...]

    def step(ki, carry, masked):
        m, l, acc = carry
        start = pl.multiple_of(ki * blk, blk)
        s = _dot_nt(q, k_ref[pl.ds(start, blk), :])
        if masked:
            row = lax.broadcasted_iota(jnp.int32, (blk, blk), 0) // CHUNK
            col = lax.broadcasted_iota(jnp.int32, (blk, blk), 1) // CHUNK
            s = jnp.where(col <= row, s, NEG_INF)
        m_new = jnp.maximum(m, jnp.max(s, axis=-1, keepdims=True))
        p = jnp.exp(s - m_new)
        alpha = jnp.exp(m - m_new)
        l = alpha * l + jnp.sum(p, axis=-1, keepdims=True)
        acc = alpha * acc + _dot(p.astype(BF16), v_ref[pl.ds(start, blk), :])
        return m_new, l, acc

    init = (jnp.full((blk, 1), NEG_INF, F32), jnp.zeros((blk, 1), F32),
            jnp.zeros((blk, V_HEAD), F32))
    carry = lax.fori_loop(0, qi, lambda ki, c: step(ki, c, False), init)
    _, l, acc = step(qi, carry, True)
    o_ref[...] = (acc / l).astype(BF16)


def _attention(q, k, v, *, blk=512):
    b, nh, s, _ = q.shape
    blk = min(blk, s)
    assert blk % CHUNK == 0
    return pl.pallas_call(
        functools.partial(_attn_body, blk=blk),
        grid=(b, nh, s // blk),
        in_specs=[
            pl.BlockSpec((None, None, blk, HEAD_PAD), lambda bi, h, i: (bi, h, i, 0)),
            pl.BlockSpec((None, None, s, HEAD_PAD), lambda bi, h, i: (bi, h, 0, 0)),
            pl.BlockSpec((None, None, s, V_HEAD), lambda bi, h, i: (bi, h, 0, 0)),
        ],
        out_specs=pl.BlockSpec((None, blk, V_HEAD), lambda bi, h, i: (bi, i, h)),
        out_shape=jax.ShapeDtypeStruct((b, s, nh * V_HEAD), BF16),
        compiler_params=_params(40),
        name="mla_attention",
    )(q, k, v)


CONV_HALO = 32
CONV_ROWS = 32
CONV_LANES = 256


def _conv_body(a_ref, prev_ref, w_ref, b_ref, g_ref, beta_ref, o_ref, buf_ref, y_ref):
    i = pl.program_id(1)
    tm, c = a_ref.shape
    buf_ref[CONV_HALO:CONV_HALO + tm, :] = a_ref[...]
    buf_ref[CONV_HALO + tm:, :] = jnp.zeros((SUBLANES, c), F32)

    @pl.when(i == 0)
    def _():
        buf_ref[:CONV_HALO, :] = jnp.zeros((CONV_HALO, c), F32)

    @pl.when(i > 0)
    def _():
        buf_ref[:CONV_HALO, :] = prev_ref[...]

    first = CONV_HALO - (CONV_WIDTH - 1)

    def rows(r, carry):
        r0 = pl.multiple_of(r * CONV_ROWS, CONV_ROWS)
        for lb in range(c // CONV_LANES):
            lanes = slice(lb * CONV_LANES, (lb + 1) * CONV_LANES)
            y = jnp.broadcast_to(b_ref[:, lanes], (CONV_ROWS, CONV_LANES))
            for res in range(SUBLANES):
                part = None
                for off in range(first + (res - first) % SUBLANES, first + CONV_WIDTH, SUBLANES):
                    tap = off - first
                    win = buf_ref[pl.ds(r0 + (off - res), CONV_ROWS + SUBLANES), lanes]
                    term = w_ref[tap:tap + 1, lanes] * win
                    part = term if part is None else part + term
                y = y + part[res:res + CONV_ROWS]
            y_ref[pl.ds(r0, CONV_ROWS), lanes] = y
        y = y_ref[pl.ds(r0, CONV_ROWS), :]
        mu = jnp.mean(y, axis=-1, keepdims=True)
        yc = y - mu
        var = jnp.mean(yc * yc, axis=-1, keepdims=True)
        z = yc * lax.rsqrt(var + EPS) * g_ref[...] + beta_ref[...]
        o_ref[pl.ds(r0, CONV_ROWS), :] = (z * jax.nn.sigmoid(z)).astype(BF16)
        return carry

    lax.fori_loop(0, tm // CONV_ROWS, rows, 0)


def _conv(a, w, bias, g, beta, *, tm=512):
    b, s, c = a.shape
    tm = min(tm, s)
    assert tm % CONV_ROWS == 0 and tm % CONV_HALO == 0 and c % CONV_LANES == 0
    per = tm // CONV_HALO

    def const(shape):
        return pl.BlockSpec(shape, lambda bi, i: (0, 0))

    return pl.pallas_call(
        _conv_body,
        grid=(b, s // tm),
        in_specs=[
            pl.BlockSpec((None, tm, c), lambda bi, i: (bi, i, 0)),
            pl.BlockSpec((None, CONV_HALO, c), lambda bi, i: (bi, jnp.maximum(i * per - 1, 0), 0)),
            const(w.shape), const((1, c)), const((1, c)), const((1, c)),
        ],
        out_specs=pl.BlockSpec((None, tm, c), lambda bi, i: (bi, i, 0)),
        out_shape=jax.ShapeDtypeStruct((b, s, c), BF16),
        scratch_shapes=[pltpu.VMEM((CONV_HALO + tm + SUBLANES, c), F32), pltpu.VMEM((tm, c), F32)],
        compiler_params=_params(32),
        name="conv_module",
    )(a, a, w, bias, g, beta)


def _outproj_body(x_ref, yc_ref, ym_ref, wt_ref, wb_ref, o_ref):
    o_ref[...] = x_ref[...] + _dot(yc_ref[...], wt_ref[...]) + _dot(ym_ref[...], wb_ref[...])


def _outproj(x, yc, ym, wt, wb, *, tm=512):
    b, s, d = x.shape
    tm = min(tm, s)

    def tile(c):
        return pl.BlockSpec((None, tm, c), lambda bi, i: (bi, i, 0))

    def const(shape):
        return pl.BlockSpec(shape, lambda bi, i: (0, 0))

    return pl.pallas_call(
        _outproj_body,
        grid=(b, s // tm),
        in_specs=[tile(d), tile(yc.shape[-1]), tile(ym.shape[-1]), const(wt.shape), const(wb.shape)],
        out_specs=tile(d),
        out_shape=jax.ShapeDtypeStruct((b, s, d), F32),
        compiler_params=_params(48),
        name="outproj",
    )(x, yc, ym, wt, wb)


def _memkv_body(mem_ref, g_ref, wk_ref, wv_ref, gk_ref, k_ref, v_ref):
    m = _rms(mem_ref[...], g_ref[...]).astype(BF16)
    k_ref[...] = _rms(_dot(m, wk_ref[...]), gk_ref[...]).astype(BF16)
    v_ref[...] = _dot(m, wv_ref[...]).astype(BF16)


def _memkv(mem, g, wk, wv, gk):
    rows, d = mem.shape
    hd = d // X_HEADS
    col = pl.BlockSpec((d, hd), lambda h: (0, h))
    out = pl.BlockSpec((rows, hd), lambda h: (0, h))
    shape = jax.ShapeDtypeStruct((rows, d), BF16)
    return pl.pallas_call(
        _memkv_body,
        grid=(X_HEADS,),
        in_specs=[pl.BlockSpec((rows, d), lambda h: (0, 0)), pl.BlockSpec((1, d), lambda h: (0, 0)),
                  col, col, pl.BlockSpec((1, hd), lambda h: (0, 0))],
        out_specs=[out, out],
        out_shape=[shape, shape],
        compiler_params=_params(32),
        name="mem_kv",
    )(mem, g, wk, wv, gk)


def _cross_body(x_ref, g_ref, wq_ref, gq_ref, k_ref, v_ref, wo_ref, o_ref, att_ref):
    x = x_ref[...]
    h = _rms(x, g_ref[...]).astype(BF16)
    hd = x.shape[-1] // X_HEADS
    scale = hd ** -0.5
    for i in range(X_HEADS):
        cols = slice(i * hd, (i + 1) * hd)
        q = _dot(h, wq_ref[:, cols])
        ss = jnp.mean(q * q, axis=-1, keepdims=True)
        qn = (q * (lax.rsqrt(ss + EPS) * scale) * gq_ref[...]).astype(BF16)
        s = _dot_nt(qn, k_ref[:, cols])
        p = jnp.exp(s - jnp.max(s, axis=-1, keepdims=True))
        l = jnp.sum(p, axis=-1, keepdims=True)
        att_ref[:, cols] = (_dot(p.astype(BF16), v_ref[:, cols]) / l).astype(BF16)
    o_ref[...] = x + _dot(att_ref[...], wo_ref[...])


def _cross(x, g, wq, gq, k, v, wo, *, tm=256):
    b, s, d = x.shape
    tm = min(tm, s)
    m = k.shape[1]
    hd = d // X_HEADS

    def tile(c):
        return pl.BlockSpec((None, tm, c), lambda bi, i: (bi, i, 0))

    def const(shape):
        return pl.BlockSpec(shape, lambda bi, i: (0, 0))

    mem = pl.BlockSpec((None, m, d), lambda bi, i: (bi, 0, 0))
    return pl.pallas_call(
        _cross_body,
        grid=(b, s // tm),
        in_specs=[tile(d), const((1, d)), const(wq.shape), const((1, hd)), mem, mem, const(wo.shape)],
        out_specs=tile(d),
        out_shape=jax.ShapeDtypeStruct((b, s, d), F32),
        scratch_shapes=[pltpu.VMEM((tm, d), BF16)],
        compiler_params=_params(56),
        name="cross_attention",
    )(x, g, wq, gq, k, v, wo)


def _pad_cols(w, heads, width, padded):
    k = w.shape[0]
    w = w.reshape(k, heads, width)
    return jnp.pad(w, ((0, 0), (0, 0), (0, padded - width))).reshape(k, heads * padded)


def kernel(x, mem, positions, ffn1_norm, ffn1_w_gate, ffn1_w_up, ffn1_w_down, mix_norm, w_in, conv_w, conv_b, conv_ln_g, conv_ln_b, q_a_norm, w_q_b, kv_a_norm, w_kv_b, q_norm, k_norm, w_out, cross_norm, mem_norm, w_cq, w_ck, w_cv, cq_norm, ck_norm, w_co, ffn2_norm, ffn2_w_gate, ffn2_w_up, ffn2_w_down):
    b, s, d = x.shape
    n_mem = mem.shape[1]
    depth = w_in.shape[0]
    cc = conv_w.shape[-1]
    o_q, o_kv, o_pe = 2 * cc, 2 * cc + Q_LORA, 2 * cc + Q_LORA + KV_LORA

    def row(v):
        return v.astype(F32)[None]

    tabs = _rope_tables(positions)
    mem2d = mem.reshape(b * n_mem, d)

    for l in range(depth):
        x = _ffn(x.reshape(b * s, d), row(ffn1_norm[l]), ffn1_w_gate[l].astype(BF16),
                 ffn1_w_up[l].astype(BF16), ffn1_w_down[l].astype(BF16)).reshape(b, s, d)

        wi = w_in[l]
        wpe = jnp.pad(wi[:, o_pe:], ((0, 0), (0, LANES - QK_ROPE)))
        a, cq, ckv, kpe = _inproj(
            x, row(mix_norm[l]), wi[:, :o_q].astype(BF16), wi[:, o_q:o_kv].astype(BF16),
            wi[:, o_kv:o_pe].astype(BF16), wpe.astype(BF16), row(q_a_norm[l]), row(kv_a_norm[l]))

        wkv = w_kv_b[l].reshape(KV_LORA, MLA_HEADS, QK_NOPE + V_HEAD)
        wk = wkv[:, :, :QK_NOPE].reshape(KV_LORA, MLA_HEADS * QK_NOPE)
        wv = wkv[:, :, QK_NOPE:].reshape(KV_LORA, MLA_HEADS * V_HEAD)
        gq = jnp.pad(q_norm[l], (0, HEAD_PAD - QK_HEAD))
        gkp = jnp.pad(k_norm[l][QK_NOPE:], (0, LANES - QK_ROPE))
        q, k, v = _qkv(cq, ckv, kpe, tabs,
                       _pad_cols(w_q_b[l], MLA_HEADS, QK_HEAD, HEAD_PAD).astype(BF16),
                       wk.astype(BF16), wv.astype(BF16), row(gq), row(k_norm[l][:QK_NOPE]), row(gkp))
        y_mla = _attention(q, k, v)

        cw = jnp.pad(conv_w[l], ((0, CONV_HALO - CONV_WIDTH), (0, 0)))
        y_conv = _conv(a, cw, row(conv_b[l]), row(conv_ln_g[l]), row(conv_ln_b[l]))

        wo = w_out[l].astype(BF16)
        x = _outproj(x, y_conv, y_mla, wo[:cc], wo[cc:])

        km, vm = _memkv(mem2d, row(mem_norm[l]), w_ck[l].astype(BF16), w_cv[l].astype(BF16),
                        row(ck_norm[l]))
        x = _cross(x, row(cross_norm[l]), w_cq[l].astype(BF16), row(cq_norm[l]),
                   km.reshape(b, n_mem, d), vm.reshape(b, n_mem, d), w_co[l].astype(BF16))

        x = _ffn(x.reshape(b * s, d), row(ffn2_norm[l]), ffn2_w_gate[l].astype(BF16),
                 ffn2_w_up[l].astype(BF16), ffn2_w_down[l].astype(BF16)).reshape(b, s, d)
    return x
```

```python
import functools

import jax
import jax.numpy as jnp
from jax import lax
from jax.experimental import pallas as pl
from jax.experimental.pallas import tpu as pltpu

CHUNK = 64
CONV_WIDTH = 31
MLA_HEADS = 8
QK_NOPE = 128
QK_ROPE = 64
QK_HEAD = QK_NOPE + QK_ROPE
V_HEAD = 128
Q_LORA = 768
KV_LORA = 256
ROPE_THETA = 10000.0
X_HEADS = 4
EPS = 1e-6
NEG_INF = -1e30

LANES = 128
SUBLANES = 8
VMEM_BYTES = 64 * 1024 * 1024
HEAD_PAD = 2 * LANES

F32 = jnp.float32
BF16 = jnp.bfloat16


def _params(vmem_mib):
    assert vmem_mib * 1024 * 1024 <= VMEM_BYTES
    return pltpu.CompilerParams(vmem_limit_bytes=vmem_mib * 1024 * 1024)


def _rms(x, g):
    ms = jnp.mean(x * x, axis=-1, keepdims=True)
    return x * lax.rsqrt(ms + EPS) * g


def _dot(a, b):
    return jnp.dot(a, b, preferred_element_type=F32)


def _dot_nt(a, b):
    return lax.dot_general(a, b, (((1,), (1,)), ((), ())), preferred_element_type=F32)


def _ffn_body(x_ref, g_ref, wg_ref, wu_ref, wd_ref, o_ref, h_ref, acc_ref):
    j = pl.program_id(1)

    @pl.when(j == 0)
    def _():
        h_ref[...] = _rms(x_ref[...], g_ref[...]).astype(BF16)
        acc_ref[...] = jnp.zeros_like(acc_ref)

    h = h_ref[...]
    gate = _dot(h, wg_ref[...])
    up = _dot(h, wu_ref[...])
    mid = (gate * jax.nn.sigmoid(gate) * up).astype(BF16)
    acc_ref[...] += _dot(mid, wd_ref[...])

    @pl.when(j == pl.num_programs(1) - 1)
    def _():
        o_ref[...] = x_ref[...] + 0.5 * acc_ref[...]


def _ffn(x, g, wg, wu, wd, *, tm=512, tf=512):
    t, d = x.shape
    f = wg.shape[1]
    tm, tf = min(tm, t), min(tf, f)
    return pl.pallas_call(
        _ffn_body,
        grid=(t // tm, f // tf),
        in_specs=[
            pl.BlockSpec((tm, d), lambda i, j: (i, 0)),
            pl.BlockSpec((1, d), lambda i, j: (0, 0)),
            pl.BlockSpec((d, tf), lambda i, j: (0, j)),
            pl.BlockSpec((d, tf), lambda i, j: (0, j)),
            pl.BlockSpec((tf, d), lambda i, j: (j, 0)),
        ],
        out_specs=pl.BlockSpec((tm, d), lambda i, j: (i, 0)),
        out_shape=jax.ShapeDtypeStruct((t, d), F32),
        scratch_shapes=[pltpu.VMEM((tm, d), BF16), pltpu.VMEM((tm, d), F32)],
        compiler_params=_params(48),
        name="ffn",
    )(x, g, wg, wu, wd)


def _rope_tab_body(pos_ref, freq_ref, mc_ref, m1_ref, m2_ref, c_ref, s1_ref, s2_ref):
    ang = pos_ref[...].astype(F32) * freq_ref[...]
    cos, sin = jnp.cos(ang), jnp.sin(ang)
    c_ref[...] = cos * mc_ref[...]
    s1_ref[...] = sin * m1_ref[...]
    s2_ref[...] = sin * m2_ref[...]


def _rope_tables(positions, *, ts=1024):
    b, s = positions.shape
    ts = min(ts, s)
    half = QK_ROPE // 2
    inv_freq = 1.0 / (ROPE_THETA ** (jnp.arange(0, QK_ROPE, 2, dtype=F32) / QK_ROPE))
    zeros = jnp.zeros((LANES - QK_ROPE,), F32)
    ones, zhalf = jnp.ones((half,), F32), jnp.zeros((half,), F32)
    freq = jnp.concatenate([inv_freq, inv_freq, zeros])[None]
    mc = jnp.concatenate([ones, ones, zeros])[None]
    m1 = jnp.concatenate([-ones, zhalf, zeros])[None]
    m2 = jnp.concatenate([zhalf, ones, zeros])[None]
    row = pl.BlockSpec((1, LANES), lambda bi, i: (0, 0))
    tab = pl.BlockSpec((None, ts, LANES), lambda bi, i: (bi, i, 0))
    shape = jax.ShapeDtypeStruct((b, s, LANES), F32)
    return pl.pallas_call(
        _rope_tab_body,
        grid=(b, s // ts),
        in_specs=[pl.BlockSpec((None, ts, 1), lambda bi, i: (bi, i, 0)), row, row, row, row],
        out_specs=[tab, tab, tab],
        out_shape=[shape, shape, shape],
        name="rope_tables",
    )(positions[..., None], freq, mc, m1, m2)


def _rope(r, c, s1, s2):
    return r * c + pltpu.roll(r, LANES - QK_ROPE // 2, 1) * s1 + pltpu.roll(r, QK_ROPE // 2, 1) * s2


def _inproj_body(x_ref, g_ref, wc_ref, wq_ref, wkv_ref, wpe_ref, gq_ref, gkv_ref,
                 a_ref, cq_ref, ckv_ref, kpe_ref):
    h = _rms(x_ref[...], g_ref[...]).astype(BF16)
    cc = a_ref.shape[-1]
    a_ref[...] = _dot(h, wc_ref[:, :cc]) * jax.nn.sigmoid(_dot(h, wc_ref[:, cc:]))
    cq_ref[...] = _rms(_dot(h, wq_ref[...]), gq_ref[...]).astype(BF16)
    ckv_ref[...] = _rms(_dot(h, wkv_ref[...]), gkv_ref[...]).astype(BF16)
    kpe_ref[...] = _dot(h, wpe_ref[...])


def _inproj(x, g, wc, wq, wkv, wpe, gq, gkv, *, tm=512):
    b, s, d = x.shape
    tm = min(tm, s)
    cc = wc.shape[1] // 2

    def const(shape):
        return pl.BlockSpec(shape, lambda bi, i: (0, 0))

    def tile(c):
        return pl.BlockSpec((None, tm, c), lambda bi, i: (bi, i, 0))

    return pl.pallas_call(
        _inproj_body,
        grid=(b, s // tm),
        in_specs=[tile(d), const((1, d)), const(wc.shape), const(wq.shape), const(wkv.shape),
                  const(wpe.shape), const((1, Q_LORA)), const((1, KV_LORA))],
        out_specs=[tile(cc), tile(Q_LORA), tile(KV_LORA), tile(LANES)],
        out_shape=[jax.ShapeDtypeStruct((b, s, cc), F32),
                   jax.ShapeDtypeStruct((b, s, Q_LORA), BF16),
                   jax.ShapeDtypeStruct((b, s, KV_LORA), BF16),
                   jax.ShapeDtypeStruct((b, s, LANES), F32)],
        compiler_params=_params(56),
        name="inproj",
    )(x, g, wc, wq, wkv, wpe, gq, gkv)


LOG2E = 1.4426950408889634
VT_ROWS = V_HEAD + 16


def _qkv_body(cq_ref, ckv_ref, kpe_ref, c_ref, s1_ref, s2_ref, wq_ref, wk_ref, wvt_ref,
              gq_ref, gkn_ref, gkp_ref, q_ref, k_ref, vt_ref):
    c, s1, s2 = c_ref[...], s1_ref[...], s2_ref[...]
    scale = QK_HEAD ** -0.5 * LOG2E
    cq, ckv = cq_ref[...], ckv_ref[...]

    kpe = kpe_ref[...]
    kpe_ss = jnp.sum(kpe * kpe, axis=-1, keepdims=True)
    kpe_rot = _rope(kpe * gkp_ref[...], c, s1, s2)

    for hd in range(MLA_HEADS):
        qh = _dot(cq, wq_ref[:, hd * HEAD_PAD:(hd + 1) * HEAD_PAD])
        ss = jnp.sum(qh * qh, axis=-1, keepdims=True)
        qn = qh * (lax.rsqrt(ss * (1.0 / QK_HEAD) + EPS) * scale) * gq_ref[...]
        q_ref[hd, :LANES, :] = qn[:, :LANES].T.astype(BF16)
        q_ref[hd, LANES:, :] = _rope(qn[:, LANES:], c, s1, s2).T.astype(BF16)

        kn = _dot(ckv, wk_ref[:, hd * QK_NOPE:(hd + 1) * QK_NOPE])
        ss = jnp.sum(kn * kn, axis=-1, keepdims=True) + kpe_ss
        rinv = lax.rsqrt(ss * (1.0 / QK_HEAD) + EPS)
        k_ref[hd, :, :LANES] = (kn * rinv * gkn_ref[...]).astype(BF16)
        k_ref[hd, :, LANES:] = (kpe_rot * rinv).astype(BF16)

        vt = _dot_nt(wvt_ref[hd * V_HEAD:(hd + 1) * V_HEAD, :], ckv)
        vt_ref[hd, :V_HEAD, :] = vt.astype(BF16)
        vt_ref[hd, V_HEAD:, :] = jnp.ones((VT_ROWS - V_HEAD, vt.shape[1]), BF16)


def _qkv(cq, ckv, kpe, tabs, wq, wk, wvt, gq, gkn, gkp, *, tm):
    b, s, _ = cq.shape
    assert s % tm == 0

    def const(shape):
        return pl.BlockSpec(shape, lambda bi, i: (0, 0))

    def tile(c):
        return pl.BlockSpec((None, tm, c), lambda bi, i: (bi, i, 0))

    def heads(c):
        return pl.BlockSpec((None, MLA_HEADS, tm, c), lambda bi, i: (bi, 0, i, 0))

    return pl.pallas_call(
        _qkv_body,
        grid=(b, s // tm),
        in_specs=[tile(Q_LORA), tile(KV_LORA), tile(LANES), tile(LANES), tile(LANES), tile(LANES),
                  const(wq.shape), const(wk.shape), const(wvt.shape),
                  const((1, HEAD_PAD)), const((1, LANES)), const((1, LANES))],
        out_specs=[pl.BlockSpec((None, MLA_HEADS, HEAD_PAD, tm), lambda bi, i: (bi, 0, 0, i)),
                   heads(HEAD_PAD),
                   pl.BlockSpec((None, MLA_HEADS, None, VT_ROWS, tm), lambda bi, i: (bi, 0, i, 0, 0))],
        out_shape=[jax.ShapeDtypeStruct((b, MLA_HEADS, HEAD_PAD, s), BF16),
                   jax.ShapeDtypeStruct((b, MLA_HEADS, s, HEAD_PAD), BF16),
                   jax.ShapeDtypeStruct((b, MLA_HEADS, s // tm, VT_ROWS, tm), BF16)],
        compiler_params=_params(40),
        name="qkv",
    )(cq, ckv, kpe, *tabs, wq, wk, wvt, gq, gkn, gkp)


ATTN_QW = 256
ATTN_BLOCK = 1024
ATTN_AHEAD = 2


def _colmax(x):
    while x.shape[0] >= SUBLANES * SUBLANES:
        x = jnp.max(x.reshape(SUBLANES, x.shape[0] // SUBLANES, x.shape[1]), axis=0)
    return jnp.max(x, axis=0, keepdims=True)


def _attn_body(qt_ref, k_ref, vt_ref, o_ref, s_ref, m_ref, acc_ref, *, blk):
    qi = pl.program_id(2)
    chains = blk // ATTN_QW
    m_ref[...] = jnp.full(m_ref.shape, NEG_INF, F32)
    acc_ref[...] = jnp.zeros(acc_ref.shape, F32)

    def nkeys(c, diag):
        return (c + 1) * ATTN_QW if diag else blk

    def scores(ki, c, diag):
        n = nkeys(c, diag)
        start = pl.multiple_of(ki * blk, blk)
        s_ref[c, :n, :] = _dot(k_ref[pl.ds(start, n), :], qt_ref[:, c * ATTN_QW:(c + 1) * ATTN_QW])

    def softmax_pv(ki, c, diag):
        n = nkeys(c, diag)
        st = s_ref[c, :n, :]
        if diag:
            key = lax.broadcasted_iota(jnp.int32, st.shape, 0) // CHUNK
            qry = (lax.broadcasted_iota(jnp.int32, st.shape, 1) + c * ATTN_QW) // CHUNK
            st = jnp.where(key <= qry, st, NEG_INF)
        m_old = m_ref[c]
        m_new = jnp.maximum(m_old, _colmax(st))
        p = jnp.exp2(st - m_new).astype(BF16)
        acc_ref[c] = jnp.exp2(m_old - m_new) * acc_ref[c] + _dot(vt_ref[ki, :, :n], p)
        m_ref[c] = m_new

    for c in range(ATTN_AHEAD):
        scores(0, c, False)

    def unmasked(ki, carry):
        for c in range(chains):
            ahead = c + ATTN_AHEAD
            scores(ki + ahead // chains, ahead % chains, False)
            softmax_pv(ki, c, False)
        return carry

    lax.fori_loop(0, qi, unmasked, 0)
    for c in range(chains):
        if c + ATTN_AHEAD < chains:
            scores(qi, c + ATTN_AHEAD, True)
        softmax_pv(qi, c, True)
    for c in range(chains):
        acc = acc_ref[c]
        out = acc[:V_HEAD] / acc[V_HEAD:V_HEAD + 1]
        o_ref[c * ATTN_QW:(c + 1) * ATTN_QW, :] = out.T.astype(BF16)


def _attention(qt, k, vt, *, blk):
    b, nh, s, _ = k.shape
    assert blk % CHUNK == 0 and blk % ATTN_QW == 0 and vt.shape[2:] == (s // blk, VT_ROWS, blk)
    chains = blk // ATTN_QW
    return pl.pallas_call(
        functools.partial(_attn_body, blk=blk),
        grid=(b, nh, s // blk),
        in_specs=[
            pl.BlockSpec((None, None, HEAD_PAD, blk), lambda bi, h, i: (bi, h, 0, i)),
            pl.BlockSpec((None, None, s, HEAD_PAD), lambda bi, h, i: (bi, h, 0, 0)),
            pl.BlockSpec((None, None, s // blk, VT_ROWS, blk), lambda bi, h, i: (bi, h, 0, 0, 0)),
        ],
        out_specs=pl.BlockSpec((None, blk, V_HEAD), lambda bi, h, i: (bi, i, h)),
        out_shape=jax.ShapeDtypeStruct((b, s, nh * V_HEAD), BF16),
        scratch_shapes=[pltpu.VMEM((chains, blk, ATTN_QW), F32),
                        pltpu.VMEM((chains, 1, ATTN_QW), F32),
                        pltpu.VMEM((chains, VT_ROWS, ATTN_QW), F32)],
        compiler_params=_params(40),
        name="mla_attention",
    )(qt, k, vt)


CONV_HALO = 32
CONV_ROWS = 32
CONV_LANES = 256


def _conv_body(a_ref, prev_ref, w_ref, b_ref, g_ref, beta_ref, o_ref, buf_ref, y_ref):
    i = pl.program_id(1)
    tm, c = a_ref.shape
    buf_ref[CONV_HALO:CONV_HALO + tm, :] = a_ref[...]
    buf_ref[CONV_HALO + tm:, :] = jnp.zeros((SUBLANES, c), F32)

    @pl.when(i == 0)
    def _():
        buf_ref[:CONV_HALO, :] = jnp.zeros((CONV_HALO, c), F32)

    @pl.when(i > 0)
    def _():
        buf_ref[:CONV_HALO, :] = prev_ref[...]

    first = CONV_HALO - (CONV_WIDTH - 1)

    def rows(r, carry):
        r0 = pl.multiple_of(r * CONV_ROWS, CONV_ROWS)
        for lb in range(c // CONV_LANES):
            lanes = slice(lb * CONV_LANES, (lb + 1) * CONV_LANES)
            y = jnp.broadcast_to(b_ref[:, lanes], (CONV_ROWS, CONV_LANES))
            for res in range(SUBLANES):
                part = None
                for off in range(first + (res - first) % SUBLANES, first + CONV_WIDTH, SUBLANES):
                    tap = off - first
                    win = buf_ref[pl.ds(r0 + (off - res), CONV_ROWS + SUBLANES), lanes]
                    term = w_ref[tap:tap + 1, lanes] * win
                    part = term if part is None else part + term
                y = y + part[res:res + CONV_ROWS]
            y_ref[pl.ds(r0, CONV_ROWS), lanes] = y
        y = y_ref[pl.ds(r0, CONV_ROWS), :]
        mu = jnp.mean(y, axis=-1, keepdims=True)
        yc = y - mu
        var = jnp.mean(yc * yc, axis=-1, keepdims=True)
        z = yc * lax.rsqrt(var + EPS) * g_ref[...] + beta_ref[...]
        o_ref[pl.ds(r0, CONV_ROWS), :] = (z * jax.nn.sigmoid(z)).astype(BF16)
        return carry

    lax.fori_loop(0, tm // CONV_ROWS, rows, 0)


def _conv(a, w, bias, g, beta, *, tm=512):
    b, s, c = a.shape
    tm = min(tm, s)
    assert tm % CONV_ROWS == 0 and tm % CONV_HALO == 0 and c % CONV_LANES == 0
    per = tm // CONV_HALO

    def const(shape):
        return pl.BlockSpec(shape, lambda bi, i: (0, 0))

    return pl.pallas_call(
        _conv_body,
        grid=(b, s // tm),
        in_specs=[
            pl.BlockSpec((None, tm, c), lambda bi, i: (bi, i, 0)),
            pl.BlockSpec((None, CONV_HALO, c), lambda bi, i: (bi, jnp.maximum(i * per - 1, 0), 0)),
            const(w.shape), const((1, c)), const((1, c)), const((1, c)),
        ],
        out_specs=pl.BlockSpec((None, tm, c), lambda bi, i: (bi, i, 0)),
        out_shape=jax.ShapeDtypeStruct((b, s, c), BF16),
        scratch_shapes=[pltpu.VMEM((CONV_HALO + tm + SUBLANES, c), F32), pltpu.VMEM((tm, c), F32)],
        compiler_params=_params(32),
        name="conv_module",
    )(a, a, w, bias, g, beta)


def _outproj_body(x_ref, yc_ref, ym_ref, wt_ref, wb_ref, o_ref):
    o_ref[...] = x_ref[...] + _dot(yc_ref[...], wt_ref[...]) + _dot(ym_ref[...], wb_ref[...])


def _outproj(x, yc, ym, wt, wb, *, tm=512):
    b, s, d = x.shape
    tm = min(tm, s)

    def tile(c):
        return pl.BlockSpec((None, tm, c), lambda bi, i: (bi, i, 0))

    def const(shape):
        return pl.BlockSpec(shape, lambda bi, i: (0, 0))

    return pl.pallas_call(
        _outproj_body,
        grid=(b, s // tm),
        in_specs=[tile(d), tile(yc.shape[-1]), tile(ym.shape[-1]), const(wt.shape), const(wb.shape)],
        out_specs=tile(d),
        out_shape=jax.ShapeDtypeStruct((b, s, d), F32),
        compiler_params=_params(48),
        name="outproj",
    )(x, yc, ym, wt, wb)


def _memkv_body(mem_ref, g_ref, wk_ref, wv_ref, gk_ref, k_ref, v_ref):
    m = _rms(mem_ref[...], g_ref[...]).astype(BF16)
    k_ref[...] = _rms(_dot(m, wk_ref[...]), gk_ref[...]).astype(BF16)
    v_ref[...] = _dot(m, wv_ref[...]).astype(BF16)


def _memkv(mem, g, wk, wv, gk):
    rows, d = mem.shape
    hd = d // X_HEADS
    col = pl.BlockSpec((d, hd), lambda h: (0, h))
    out = pl.BlockSpec((rows, hd), lambda h: (0, h))
    shape = jax.ShapeDtypeStruct((rows, d), BF16)
    return pl.pallas_call(
        _memkv_body,
        grid=(X_HEADS,),
        in_specs=[pl.BlockSpec((rows, d), lambda h: (0, 0)), pl.BlockSpec((1, d), lambda h: (0, 0)),
                  col, col, pl.BlockSpec((1, hd), lambda h: (0, 0))],
        out_specs=[out, out],
        out_shape=[shape, shape],
        compiler_params=_params(32),
        name="mem_kv",
    )(mem, g, wk, wv, gk)


def _cross_body(x_ref, g_ref, wq_ref, gq_ref, k_ref, v_ref, wo_ref, o_ref, att_ref):
    x = x_ref[...]
    h = _rms(x, g_ref[...]).astype(BF16)
    hd = x.shape[-1] // X_HEADS
    scale = hd ** -0.5
    for i in range(X_HEADS):
        cols = slice(i * hd, (i + 1) * hd)
        q = _dot(h, wq_ref[:, cols])
        ss = jnp.mean(q * q, axis=-1, keepdims=True)
        qn = (q * (lax.rsqrt(ss + EPS) * scale) * gq_ref[...]).astype(BF16)
        s = _dot_nt(qn, k_ref[:, cols])
        p = jnp.exp(s - jnp.max(s, axis=-1, keepdims=True))
        l = jnp.sum(p, axis=-1, keepdims=True)
        att_ref[:, cols] = (_dot(p.astype(BF16), v_ref[:, cols]) / l).astype(BF16)
    o_ref[...] = x + _dot(att_ref[...], wo_ref[...])


def _cross(x, g, wq, gq, k, v, wo, *, tm=256):
    b, s, d = x.shape
    tm = min(tm, s)
    m = k.shape[1]
    hd = d // X_HEADS

    def tile(c):
        return pl.BlockSpec((None, tm, c), lambda bi, i: (bi, i, 0))

    def const(shape):
        return pl.BlockSpec(shape, lambda bi, i: (0, 0))

    mem = pl.BlockSpec((None, m, d), lambda bi, i: (bi, 0, 0))
    return pl.pallas_call(
        _cross_body,
        grid=(b, s // tm),
        in_specs=[tile(d), const((1, d)), const(wq.shape), const((1, hd)), mem, mem, const(wo.shape)],
        out_specs=tile(d),
        out_shape=jax.ShapeDtypeStruct((b, s, d), F32),
        scratch_shapes=[pltpu.VMEM((tm, d), BF16)],
        compiler_params=_params(56),
        name="cross_attention",
    )(x, g, wq, gq, k, v, wo)


def _pad_cols(w, heads, width, padded):
    k = w.shape[0]
    w = w.reshape(k, heads, width)
    return jnp.pad(w, ((0, 0), (0, 0), (0, padded - width))).reshape(k, heads * padded)


def kernel(x, mem, positions, ffn1_norm, ffn1_w_gate, ffn1_w_up, ffn1_w_down, mix_norm, w_in, conv_w, conv_b, conv_ln_g, conv_ln_b, q_a_norm, w_q_b, kv_a_norm, w_kv_b, q_norm, k_norm, w_out, cross_norm, mem_norm, w_cq, w_ck, w_cv, cq_norm, ck_norm, w_co, ffn2_norm, ffn2_w_gate, ffn2_w_up, ffn2_w_down):
    b, s, d = x.shape
    n_mem = mem.shape[1]
    depth = w_in.shape[0]
    cc = conv_w.shape[-1]
    o_q, o_kv, o_pe = 2 * cc, 2 * cc + Q_LORA, 2 * cc + Q_LORA + KV_LORA

    def row(v):
        return v.astype(F32)[None]

    tabs = _rope_tables(positions)
    mem2d = mem.reshape(b * n_mem, d)
    attn_blk = min(ATTN_BLOCK, s)

    for l in range(depth):
        x = _ffn(x.reshape(b * s, d), row(ffn1_norm[l]), ffn1_w_gate[l].astype(BF16),
                 ffn1_w_up[l].astype(BF16), ffn1_w_down[l].astype(BF16)).reshape(b, s, d)

        wi = w_in[l]
        wpe = jnp.pad(wi[:, o_pe:], ((0, 0), (0, LANES - QK_ROPE)))
        a, cq, ckv, kpe = _inproj(
            x, row(mix_norm[l]), wi[:, :o_q].astype(BF16), wi[:, o_q:o_kv].astype(BF16),
            wi[:, o_kv:o_pe].astype(BF16), wpe.astype(BF16), row(q_a_norm[l]), row(kv_a_norm[l]))

        wkv = w_kv_b[l].reshape(KV_LORA, MLA_HEADS, QK_NOPE + V_HEAD)
        wk = wkv[:, :, :QK_NOPE].reshape(KV_LORA, MLA_HEADS * QK_NOPE)
        wvt = wkv[:, :, QK_NOPE:].reshape(KV_LORA, MLA_HEADS * V_HEAD).T
        gq = jnp.pad(q_norm[l], (0, HEAD_PAD - QK_HEAD))
        gkp = jnp.pad(k_norm[l][QK_NOPE:], (0, LANES - QK_ROPE))
        q, k, vt = _qkv(cq, ckv, kpe, tabs,
                        _pad_cols(w_q_b[l], MLA_HEADS, QK_HEAD, HEAD_PAD).astype(BF16),
                        wk.astype(BF16), wvt.astype(BF16), row(gq), row(k_norm[l][:QK_NOPE]),
                        row(gkp), tm=attn_blk)
        y_mla = _attention(q, k, vt, blk=attn_blk)

        cw = jnp.pad(conv_w[l], ((0, CONV_HALO - CONV_WIDTH), (0, 0)))
        y_conv = _conv(a, cw, row(conv_b[l]), row(conv_ln_g[l]), row(conv_ln_b[l]))

        wo = w_out[l].astype(BF16)
        x = _outproj(x, y_conv, y_mla, wo[:cc], wo[cc:])

        km, vm = _memkv(mem2d, row(mem_norm[l]), w_ck[l].astype(BF16), w_cv[l].astype(BF16),
                        row(ck_norm[l]))
        x = _cross(x, row(cross_norm[l]), w_cq[l].astype(BF16), row(cq_norm[l]),
                   km.reshape(b, n_mem, d), vm.reshape(b, n_mem, d), w_co[l].astype(BF16))

        x = _ffn(x.reshape(b * s, d), row(ffn2_norm[l]), ffn2_w_gate[l].astype(BF16),
                 ffn2_w_up[l].astype(BF16), ffn2_w_down[l].astype(BF16)).reshape(b, s, d)
    return x
```

```python
import functools

import jax
import jax.numpy as jnp
from jax import lax
from jax.experimental import pallas as pl
from jax.experimental.pallas import tpu as pltpu

CHUNK = 64
CONV_WIDTH = 31
MLA_HEADS = 8
QK_NOPE = 128
QK_ROPE = 64
QK_HEAD = QK_NOPE + QK_ROPE
V_HEAD = 128
Q_LORA = 768
KV_LORA = 256
ROPE_THETA = 10000.0
X_HEADS = 4
EPS = 1e-6
NEG_INF = -1e30

LANES = 128
SUBLANES = 8
BF16_ROWS = 16
VMEM_BYTES = 64 * 1024 * 1024
HEAD_PAD = 2 * LANES

F32 = jnp.float32
BF16 = jnp.bfloat16


def _params(vmem_mib):
    assert vmem_mib * 1024 * 1024 <= VMEM_BYTES
    return pltpu.CompilerParams(vmem_limit_bytes=vmem_mib * 1024 * 1024)


def _rms(x, g):
    ms = jnp.mean(x * x, axis=-1, keepdims=True)
    return x * lax.rsqrt(ms + EPS) * g


def _dot(a, b):
    return jnp.dot(a, b, preferred_element_type=F32)


def _dot_nt(a, b):
    return lax.dot_general(a, b, (((1,), (1,)), ((), ())), preferred_element_type=F32)


CAST_BLOCK_BYTES = 8 * 1024 * 1024


def _cast_body(w_ref, o_ref):
    o_ref[...] = w_ref[...].astype(BF16)


def _to_bf16(w):
    layers, k, n = w.shape
    tk = k
    while tk * n * 4 > CAST_BLOCK_BYTES and tk % (2 * BF16_ROWS) == 0:
        tk //= 2
    spec = pl.BlockSpec((None, tk, n), lambda l, i: (l, i, 0))
    return pl.pallas_call(
        _cast_body,
        grid=(layers, k // tk),
        in_specs=[spec],
        out_specs=spec,
        out_shape=jax.ShapeDtypeStruct(w.shape, BF16),
        compiler_params=_params(40),
        name="cast_bf16",
    )(w)


def _ffn_body(x_ref, g_ref, wg_ref, wu_ref, wd_ref, o_ref, h_ref, acc_ref):
    j = pl.program_id(1)

    @pl.when(j == 0)
    def _():
        h_ref[...] = _rms(x_ref[...], g_ref[...]).astype(BF16)
        acc_ref[...] = jnp.zeros_like(acc_ref)

    h = h_ref[...]
    gate = _dot(h, wg_ref[...])
    up = _dot(h, wu_ref[...])
    mid = (gate * jax.nn.sigmoid(gate) * up).astype(BF16)
    acc_ref[...] += _dot(mid, wd_ref[...])

    @pl.when(j == pl.num_programs(1) - 1)
    def _():
        o_ref[...] = x_ref[...] + 0.5 * acc_ref[...]


def _ffn(x, g, wg, wu, wd, layer, *, tm=512, tf=512):
    t, d = x.shape
    f = wg.shape[-1]
    tm, tf = min(tm, t), min(tf, f)
    return pl.pallas_call(
        _ffn_body,
        grid=(t // tm, f // tf),
        in_specs=[
            pl.BlockSpec((tm, d), lambda i, j: (i, 0)),
            pl.BlockSpec((1, d), lambda i, j: (0, 0)),
            pl.BlockSpec((None, d, tf), lambda i, j: (layer, 0, j)),
            pl.BlockSpec((None, d, tf), lambda i, j: (layer, 0, j)),
            pl.BlockSpec((None, tf, d), lambda i, j: (layer, j, 0)),
        ],
        out_specs=pl.BlockSpec((tm, d), lambda i, j: (i, 0)),
        out_shape=jax.ShapeDtypeStruct((t, d), F32),
        scratch_shapes=[pltpu.VMEM((tm, d), BF16), pltpu.VMEM((tm, d), F32)],
        compiler_params=_params(48),
        name="ffn",
    )(x, g, wg, wu, wd)


def _rope_tab_body(pos_ref, freq_ref, mc_ref, m1_ref, m2_ref, c_ref, s1_ref, s2_ref):
    ang = pos_ref[...].astype(F32) * freq_ref[...]
    cos, sin = jnp.cos(ang), jnp.sin(ang)
    c_ref[...] = cos * mc_ref[...]
    s1_ref[...] = sin * m1_ref[...]
    s2_ref[...] = sin * m2_ref[...]


def _rope_tables(positions, *, ts=1024):
    b, s = positions.shape
    ts = min(ts, s)
    half = QK_ROPE // 2
    inv_freq = 1.0 / (ROPE_THETA ** (jnp.arange(0, QK_ROPE, 2, dtype=F32) / QK_ROPE))
    zeros = jnp.zeros((LANES - QK_ROPE,), F32)
    ones, zhalf = jnp.ones((half,), F32), jnp.zeros((half,), F32)
    freq = jnp.concatenate([inv_freq, inv_freq, zeros])[None]
    mc = jnp.concatenate([ones, ones, zeros])[None]
    m1 = jnp.concatenate([-ones, zhalf, zeros])[None]
    m2 = jnp.concatenate([zhalf, ones, zeros])[None]
    row = pl.BlockSpec((1, LANES), lambda bi, i: (0, 0))
    tab = pl.BlockSpec((None, ts, LANES), lambda bi, i: (bi, i, 0))
    shape = jax.ShapeDtypeStruct((b, s, LANES), F32)
    return pl.pallas_call(
        _rope_tab_body,
        grid=(b, s // ts),
        in_specs=[pl.BlockSpec((None, ts, 1), lambda bi, i: (bi, i, 0)), row, row, row, row],
        out_specs=[tab, tab, tab],
        out_shape=[shape, shape, shape],
        name="rope_tables",
    )(positions[..., None], freq, mc, m1, m2)


def _rope(r, c, s1, s2):
    return r * c + pltpu.roll(r, LANES - QK_ROPE // 2, 1) * s1 + pltpu.roll(r, QK_ROPE // 2, 1) * s2


def _inproj_body(x_ref, g_ref, w_ref, gq_ref, gkv_ref, a_ref, cq_ref, ckv_ref, kpe_ref):
    h = _rms(x_ref[...], g_ref[...]).astype(BF16)
    cc = a_ref.shape[-1]
    o_q, o_kv, o_pe = 2 * cc, 2 * cc + Q_LORA, 2 * cc + Q_LORA + KV_LORA
    a_ref[...] = _dot(h, w_ref[:, :cc]) * jax.nn.sigmoid(_dot(h, w_ref[:, cc:o_q]))
    cq_ref[...] = _rms(_dot(h, w_ref[:, o_q:o_kv]), gq_ref[...]).astype(BF16)
    ckv_ref[...] = _rms(_dot(h, w_ref[:, o_kv:o_pe]), gkv_ref[...]).astype(BF16)
    kpe_ref[:, :QK_ROPE] = _dot(h, w_ref[:, o_pe:])
    kpe_ref[:, QK_ROPE:] = jnp.zeros((kpe_ref.shape[0], LANES - QK_ROPE), F32)


def _inproj(x, g, w, layer, gq, gkv, *, tm=512):
    b, s, d = x.shape
    tm = min(tm, s)
    cols = w.shape[-1]
    cc = (cols - Q_LORA - KV_LORA - QK_ROPE) // 2

    def const(shape):
        return pl.BlockSpec(shape, lambda bi, i: (0, 0))

    def tile(c):
        return pl.BlockSpec((None, tm, c), lambda bi, i: (bi, i, 0))

    return pl.pallas_call(
        _inproj_body,
        grid=(b, s // tm),
        in_specs=[tile(d), const((1, d)),
                  pl.BlockSpec((None, d, cols), lambda bi, i: (layer, 0, 0)),
                  const((1, Q_LORA)), const((1, KV_LORA))],
        out_specs=[tile(cc), tile(Q_LORA), tile(KV_LORA), tile(LANES)],
        out_shape=[jax.ShapeDtypeStruct((b, s, cc), F32),
                   jax.ShapeDtypeStruct((b, s, Q_LORA), BF16),
                   jax.ShapeDtypeStruct((b, s, KV_LORA), BF16),
                   jax.ShapeDtypeStruct((b, s, LANES), F32)],
        compiler_params=_params(56),
        name="inproj",
    )(x, g, w, gq, gkv)


LOG2E = 1.4426950408889634
VT_ROWS = V_HEAD + 16


def _qkv_body(cq_ref, ckv_ref, kpe_ref, c_ref, s1_ref, s2_ref, wq_ref, wk_ref, wvt_ref,
              gq_ref, gkn_ref, gkp_ref, q_ref, k_ref, vt_ref):
    c, s1, s2 = c_ref[...], s1_ref[...], s2_ref[...]
    scale = QK_HEAD ** -0.5 * LOG2E
    cq, ckv = cq_ref[...], ckv_ref[...]

    kpe = kpe_ref[...]
    kpe_ss = jnp.sum(kpe * kpe, axis=-1, keepdims=True)
    kpe_rot = _rope(kpe * gkp_ref[...], c, s1, s2)

    for hd in range(MLA_HEADS):
        qh = _dot(cq, wq_ref[:, hd * HEAD_PAD:(hd + 1) * HEAD_PAD])
        ss = jnp.sum(qh * qh, axis=-1, keepdims=True)
        qn = qh * (lax.rsqrt(ss * (1.0 / QK_HEAD) + EPS) * scale) * gq_ref[...]
        q_ref[hd, :LANES, :] = qn[:, :LANES].T.astype(BF16)
        q_ref[hd, LANES:, :] = _rope(qn[:, LANES:], c, s1, s2).T.astype(BF16)

        kn = _dot(ckv, wk_ref[:, hd * QK_NOPE:(hd + 1) * QK_NOPE])
        ss = jnp.sum(kn * kn, axis=-1, keepdims=True) + kpe_ss
        rinv = lax.rsqrt(ss * (1.0 / QK_HEAD) + EPS)
        k_ref[hd, :, :LANES] = (kn * rinv * gkn_ref[...]).astype(BF16)
        k_ref[hd, :, LANES:] = (kpe_rot * rinv).astype(BF16)

        vt = _dot_nt(wvt_ref[hd * V_HEAD:(hd + 1) * V_HEAD, :], ckv)
        vt_ref[hd, :V_HEAD, :] = vt.astype(BF16)
        vt_ref[hd, V_HEAD:, :] = jnp.ones((VT_ROWS - V_HEAD, vt.shape[1]), BF16)


def _qkv(cq, ckv, kpe, tabs, wq, wk, wvt, gq, gkn, gkp, *, tm):
    b, s, _ = cq.shape
    assert s % tm == 0

    def const(shape):
        return pl.BlockSpec(shape, lambda bi, i: (0, 0))

    def tile(c):
        return pl.BlockSpec((None, tm, c), lambda bi, i: (bi, i, 0))

    def heads(c):
        return pl.BlockSpec((None, MLA_HEADS, tm, c), lambda bi, i: (bi, 0, i, 0))

    return pl.pallas_call(
        _qkv_body,
        grid=(b, s // tm),
        in_specs=[tile(Q_LORA), tile(KV_LORA), tile(LANES), tile(LANES), tile(LANES), tile(LANES),
                  const(wq.shape), const(wk.shape), const(wvt.shape),
                  const((1, HEAD_PAD)), const((1, LANES)), const((1, LANES))],
        out_specs=[pl.BlockSpec((None, MLA_HEADS, HEAD_PAD, tm), lambda bi, i: (bi, 0, 0, i)),
                   heads(HEAD_PAD),
                   pl.BlockSpec((None, MLA_HEADS, None, VT_ROWS, tm), lambda bi, i: (bi, 0, i, 0, 0))],
        out_shape=[jax.ShapeDtypeStruct((b, MLA_HEADS, HEAD_PAD, s), BF16),
                   jax.ShapeDtypeStruct((b, MLA_HEADS, s, HEAD_PAD), BF16),
                   jax.ShapeDtypeStruct((b, MLA_HEADS, s // tm, VT_ROWS, tm), BF16)],
        compiler_params=_params(40),
        name="qkv",
    )(cq, ckv, kpe, *tabs, wq, wk, wvt, gq, gkn, gkp)


ATTN_QW = 256
ATTN_BLOCK = 1024
ATTN_AHEAD = 3


def _colmax(x):
    while x.shape[0] >= SUBLANES * SUBLANES:
        x = jnp.max(x.reshape(SUBLANES, x.shape[0] // SUBLANES, x.shape[1]), axis=0)
    return jnp.max(x, axis=0, keepdims=True)


def _attn_body(qt_ref, k_ref, vt_ref, o_ref, s_ref, m_ref, acc_ref, *, blk):
    qi = pl.program_id(2)
    chains = blk // ATTN_QW
    m_ref[...] = jnp.full(m_ref.shape, NEG_INF, F32)
    acc_ref[...] = jnp.zeros(acc_ref.shape, F32)

    def nkeys(c, diag):
        return (c + 1) * ATTN_QW if diag else blk

    def scores(ki, c, diag):
        n = nkeys(c, diag)
        start = pl.multiple_of(ki * blk, blk)
        s_ref[c, :n, :] = _dot(k_ref[pl.ds(start, n), :], qt_ref[:, c * ATTN_QW:(c + 1) * ATTN_QW])

    def softmax_pv(ki, c, diag):
        n = nkeys(c, diag)
        st = s_ref[c, :n, :]
        if diag:
            key = lax.broadcasted_iota(jnp.int32, st.shape, 0) // CHUNK
            qry = (lax.broadcasted_iota(jnp.int32, st.shape, 1) + c * ATTN_QW) // CHUNK
            st = jnp.where(key <= qry, st, NEG_INF)
        m_old = m_ref[c]
        m_new = jnp.maximum(m_old, _colmax(st))
        p = jnp.exp2(st - m_new).astype(BF16)
        acc_ref[c] = jnp.exp2(m_old - m_new) * acc_ref[c] + _dot(vt_ref[ki, :, :n], p)
        m_ref[c] = m_new

    for c in range(ATTN_AHEAD):
        scores(0, c, False)

    def unmasked(ki, carry):
        for c in range(chains):
            ahead = c + ATTN_AHEAD
            scores(ki + ahead // chains, ahead % chains, False)
            softmax_pv(ki, c, False)
        return carry

    lax.fori_loop(0, qi, unmasked, 0)
    for c in range(chains):
        if c + ATTN_AHEAD < chains:
            scores(qi, c + ATTN_AHEAD, True)
        softmax_pv(qi, c, True)
    for c in range(chains):
        acc = acc_ref[c]
        out = acc[:V_HEAD] / acc[V_HEAD:V_HEAD + 1]
        o_ref[c * ATTN_QW:(c + 1) * ATTN_QW, :] = out.T.astype(BF16)


def _attention(qt, k, vt, *, blk):
    b, nh, s, _ = k.shape
    assert blk % CHUNK == 0 and blk % ATTN_QW == 0 and vt.shape[2:] == (s // blk, VT_ROWS, blk)
    chains = blk // ATTN_QW
    return pl.pallas_call(
        functools.partial(_attn_body, blk=blk),
        grid=(b, nh, s // blk),
        in_specs=[
            pl.BlockSpec((None, None, HEAD_PAD, blk), lambda bi, h, i: (bi, h, 0, i)),
            pl.BlockSpec((None, None, s, HEAD_PAD), lambda bi, h, i: (bi, h, 0, 0)),
            pl.BlockSpec((None, None, s // blk, VT_ROWS, blk), lambda bi, h, i: (bi, h, 0, 0, 0)),
        ],
        out_specs=pl.BlockSpec((None, blk, V_HEAD), lambda bi, h, i: (bi, i, h)),
        out_shape=jax.ShapeDtypeStruct((b, s, nh * V_HEAD), BF16),
        scratch_shapes=[pltpu.VMEM((chains, blk, ATTN_QW), F32),
                        pltpu.VMEM((chains, 1, ATTN_QW), F32),
                        pltpu.VMEM((chains, VT_ROWS, ATTN_QW), F32)],
        compiler_params=_params(40),
        name="mla_attention",
    )(qt, k, vt)


CONV_HALO = 32
CONV_PITCH = 4
CONV_ROWS = CONV_PITCH * SUBLANES
LN_ROWS = 256


def _conv_body(a_ref, prev_ref, w_ref, b_ref, g_ref, beta_ref, o_ref, buf_ref, y_ref):
    i = pl.program_id(1)
    tm, c = a_ref.shape
    tiles = c // LANES
    for lt in range(tiles):
        buf_ref[lt, CONV_HALO:, :] = a_ref[:, lt * LANES:(lt + 1) * LANES]

    @pl.when(i == 0)
    def _():
        buf_ref[:, :CONV_HALO, :] = jnp.zeros((tiles, CONV_HALO, LANES), F32)

    @pl.when(i > 0)
    def _():
        for lt in range(tiles):
            buf_ref[lt, :CONV_HALO, :] = prev_ref[:, lt * LANES:(lt + 1) * LANES]

    first = CONV_HALO - (CONV_WIDTH - 1)
    for lt in range(tiles):
        lanes = slice(lt * LANES, (lt + 1) * LANES)
        taps = [jnp.broadcast_to(w_ref[k:k + 1, lanes], (SUBLANES, LANES)) for k in range(CONV_WIDTH)]
        bias = jnp.broadcast_to(b_ref[:, lanes], (SUBLANES, LANES))

        def rows(r, carry, lt=lt, taps=taps, bias=bias):
            r0 = pl.multiple_of(r * CONV_ROWS, CONV_ROWS)
            acc = [bias] * CONV_PITCH
            for m in range(CONV_WIDTH + CONV_PITCH - 1):
                win = buf_ref[lt, pl.ds(r0 + first + m, SUBLANES, stride=CONV_PITCH), :]
                for j in range(CONV_PITCH):
                    if 0 <= m - j < CONV_WIDTH:
                        acc[j] = acc[j] + taps[m - j] * win
            for j in range(CONV_PITCH):
                y_ref[lt, pl.ds(r0 + j, SUBLANES, stride=CONV_PITCH), :] = acc[j]
            return carry

        lax.fori_loop(0, tm // CONV_ROWS, rows, 0)

    def norm(r, carry):
        r0 = pl.multiple_of(r * LN_ROWS, LN_ROWS)
        y = jnp.concatenate([y_ref[lt, pl.ds(r0, LN_ROWS), :] for lt in range(tiles)], axis=1)
        mu = jnp.mean(y, axis=-1, keepdims=True)
        yc = y - mu
        var = jnp.mean(yc * yc, axis=-1, keepdims=True)
        z = yc * lax.rsqrt(var + EPS) * g_ref[...] + beta_ref[...]
        o_ref[pl.ds(r0, LN_ROWS), :] = (z * jax.nn.sigmoid(z)).astype(BF16)
        return carry

    lax.fori_loop(0, tm // LN_ROWS, norm, 0)


def _conv(a, w, bias, g, beta, *, tm=512):
    b, s, c = a.shape
    tm = min(tm, s)
    assert tm % CONV_ROWS == 0 and tm % CONV_HALO == 0 and tm % LN_ROWS == 0 and c % LANES == 0
    per = tm // CONV_HALO
    tiles = c // LANES

    def const(shape):
        return pl.BlockSpec(shape, lambda bi, i: (0, 0))

    return pl.pallas_call(
        _conv_body,
        grid=(b, s // tm),
        in_specs=[
            pl.BlockSpec((None, tm, c), lambda bi, i: (bi, i, 0)),
            pl.BlockSpec((None, CONV_HALO, c), lambda bi, i: (bi, jnp.maximum(i * per - 1, 0), 0)),
            const(w.shape), const((1, c)), const((1, c)), const((1, c)),
        ],
        out_specs=pl.BlockSpec((None, tm, c), lambda bi, i: (bi, i, 0)),
        out_shape=jax.ShapeDtypeStruct((b, s, c), BF16),
        scratch_shapes=[pltpu.VMEM((tiles, CONV_HALO + tm, LANES), F32),
                        pltpu.VMEM((tiles, tm, LANES), F32)],
        compiler_params=_params(32),
        name="conv_module",
    )(a, a, w, bias, g, beta)


def _outproj_body(x_ref, yc_ref, ym_ref, wt_ref, wb_ref, o_ref):
    o_ref[...] = x_ref[...] + _dot(yc_ref[...], wt_ref[...]) + _dot(ym_ref[...], wb_ref[...])


def _outproj(x, yc, ym, w, layer, *, tm=512):
    b, s, d = x.shape
    tm = min(tm, s)
    cc = yc.shape[-1]
    assert w.shape[1] == 2 * cc

    def tile(c):
        return pl.BlockSpec((None, tm, c), lambda bi, i: (bi, i, 0))

    return pl.pallas_call(
        _outproj_body,
        grid=(b, s // tm),
        in_specs=[tile(d), tile(cc), tile(cc),
                  pl.BlockSpec((None, cc, d), lambda bi, i: (layer, 0, 0)),
                  pl.BlockSpec((None, cc, d), lambda bi, i: (layer, 1, 0))],
        out_specs=tile(d),
        out_shape=jax.ShapeDtypeStruct((b, s, d), F32),
        compiler_params=_params(48),
        name="outproj",
    )(x, yc, ym, w, w)


def _memkv_body(mem_ref, g_ref, wk_ref, wv_ref, gk_ref, k_ref, v_ref):
    m = _rms(mem_ref[...], g_ref[...]).astype(BF16)
    k_ref[...] = _rms(_dot(m, wk_ref[...].astype(BF16)), gk_ref[...]).astype(BF16)
    v_ref[...] = _dot(m, wv_ref[...].astype(BF16)).astype(BF16)


def _memkv(mem, g, wk, wv, layer, gk):
    rows, d = mem.shape
    hd = d // X_HEADS
    col = pl.BlockSpec((None, d, hd), lambda h: (layer, 0, h))
    out = pl.BlockSpec((rows, hd), lambda h: (0, h))
    shape = jax.ShapeDtypeStruct((rows, d), BF16)
    return pl.pallas_call(
        _memkv_body,
        grid=(X_HEADS,),
        in_specs=[pl.BlockSpec((rows, d), lambda h: (0, 0)), pl.BlockSpec((1, d), lambda h: (0, 0)),
                  col, col, pl.BlockSpec((1, hd), lambda h: (0, 0))],
        out_specs=[out, out],
        out_shape=[shape, shape],
        compiler_params=_params(40),
        name="mem_kv",
    )(mem, g, wk, wv, gk)


def _cross_body(x_ref, g_ref, wq_ref, gq_ref, k_ref, v_ref, wo_ref, o_ref, att_ref):
    x = x_ref[...]
    h = _rms(x, g_ref[...]).astype(BF16)
    hd = x.shape[-1] // X_HEADS
    scale = hd ** -0.5 * LOG2E
    cols = [slice(i * hd, (i + 1) * hd) for i in range(X_HEADS)]
    q = [_dot(h, wq_ref[:, c]) for c in cols]
    qn = [(qi * (lax.rsqrt(jnp.mean(qi * qi, axis=-1, keepdims=True) + EPS) * scale)
           * gq_ref[...]).astype(BF16) for qi in q]
    s = [_dot_nt(qi, k_ref[:, c]) for qi, c in zip(qn, cols)]
    p = [jnp.exp2(si - jnp.max(si, axis=-1, keepdims=True)) for si in s]
    for pi, c in zip(p, cols):
        l = jnp.sum(pi, axis=-1, keepdims=True)
        att_ref[:, c] = (_dot(pi.astype(BF16), v_ref[:, c]) / l).astype(BF16)
    o_ref[...] = x + _dot(att_ref[...], wo_ref[...])


def _cross(x, g, wq, gq, k, v, wo, layer, *, tm=512):
    b, s, d = x.shape
    tm = min(tm, s)
    m = k.shape[1]
    hd = d // X_HEADS

    def tile(c):
        return pl.BlockSpec((None, tm, c), lambda bi, i: (bi, i, 0))

    def const(shape):
        return pl.BlockSpec(shape, lambda bi, i: (0, 0))

    weight = pl.BlockSpec((None, d, d), lambda bi, i: (layer, 0, 0), pipeline_mode=pl.Buffered(1))

    mem = pl.BlockSpec((None, m, d), lambda bi, i: (bi, 0, 0))
    return pl.pallas_call(
        _cross_body,
        grid=(b, s // tm),
        in_specs=[tile(d), const((1, d)), weight, const((1, hd)), mem, mem, weight],
        out_specs=tile(d),
        out_shape=jax.ShapeDtypeStruct((b, s, d), F32),
        scratch_shapes=[pltpu.VMEM((tm, d), BF16)],
        compiler_params=_params(56),
        name="cross_attention",
    )(x, g, wq, gq, k, v, wo)


def _pad_cols(w, heads, width, padded):
    k = w.shape[0]
    w = w.reshape(k, heads, width)
    return jnp.pad(w, ((0, 0), (0, 0), (0, padded - width))).reshape(k, heads * padded)


def kernel(x, mem, positions, ffn1_norm, ffn1_w_gate, ffn1_w_up, ffn1_w_down, mix_norm, w_in, conv_w, conv_b, conv_ln_g, conv_ln_b, q_a_norm, w_q_b, kv_a_norm, w_kv_b, q_norm, k_norm, w_out, cross_norm, mem_norm, w_cq, w_ck, w_cv, cq_norm, ck_norm, w_co, ffn2_norm, ffn2_w_gate, ffn2_w_up, ffn2_w_down):
    b, s, d = x.shape
    n_mem = mem.shape[1]
    depth = w_in.shape[0]

    def row(v):
        return v.astype(F32)[None]

    tabs = _rope_tables(positions)
    mem2d = mem.reshape(b * n_mem, d)
    attn_blk = min(ATTN_BLOCK, s)
    ffn1 = [_to_bf16(w) for w in (ffn1_w_gate, ffn1_w_up, ffn1_w_down)]
    ffn2 = [_to_bf16(w) for w in (ffn2_w_gate, ffn2_w_up, ffn2_w_down)]
    w_in16, w_out16, w_cq16, w_co16 = (_to_bf16(w) for w in (w_in, w_out, w_cq, w_co))

    for l in range(depth):
        x = _ffn(x.reshape(b * s, d), row(ffn1_norm[l]), *ffn1, l).reshape(b, s, d)

        a, cq, ckv, kpe = _inproj(x, row(mix_norm[l]), w_in16, l, row(q_a_norm[l]), row(kv_a_norm[l]))

        wkv = w_kv_b[l].reshape(KV_LORA, MLA_HEADS, QK_NOPE + V_HEAD)
        wk = wkv[:, :, :QK_NOPE].reshape(KV_LORA, MLA_HEADS * QK_NOPE)
        wvt = wkv[:, :, QK_NOPE:].reshape(KV_LORA, MLA_HEADS * V_HEAD).T
        gq = jnp.pad(q_norm[l], (0, HEAD_PAD - QK_HEAD))
        gkp = jnp.pad(k_norm[l][QK_NOPE:], (0, LANES - QK_ROPE))
        q, k, vt = _qkv(cq, ckv, kpe, tabs,
                        _pad_cols(w_q_b[l], MLA_HEADS, QK_HEAD, HEAD_PAD).astype(BF16),
                        wk.astype(BF16), wvt.astype(BF16), row(gq), row(k_norm[l][:QK_NOPE]),
                        row(gkp), tm=attn_blk)
        y_mla = _attention(q, k, vt, blk=attn_blk)

        cw = jnp.pad(conv_w[l], ((0, CONV_HALO - CONV_WIDTH), (0, 0)))
        y_conv = _conv(a, cw, row(conv_b[l]), row(conv_ln_g[l]), row(conv_ln_b[l]))

        x = _outproj(x, y_conv, y_mla, w_out16, l)

        km, vm = _memkv(mem2d, row(mem_norm[l]), w_ck, w_cv, l, row(ck_norm[l]))
        x = _cross(x, row(cross_norm[l]), w_cq16, row(cq_norm[l]),
                   km.reshape(b, n_mem, d), vm.reshape(b, n_mem, d), w_co16, l)

        x = _ffn(x.reshape(b * s, d), row(ffn2_norm[l]), *ffn2, l).reshape(b, s, d)
    return x
```

```python
import functools

import jax
import jax.numpy as jnp
from jax import lax
from jax.experimental import pallas as pl
from jax.experimental.pallas import tpu as pltpu

CHUNK = 64
CONV_WIDTH = 31
MLA_HEADS = 8
QK_NOPE = 128
QK_ROPE = 64
QK_HEAD = QK_NOPE + QK_ROPE
V_HEAD = 128
Q_LORA = 768
KV_LORA = 256
ROPE_THETA = 10000.0
X_HEADS = 4
EPS = 1e-6
NEG_INF = -1e30

LANES = 128
SUBLANES = 8
BF16_ROWS = 16
VMEM_BYTES = 64 * 1024 * 1024
HEAD_PAD = 2 * LANES

F32 = jnp.float32
BF16 = jnp.bfloat16


def _params(vmem_mib):
    assert vmem_mib * 1024 * 1024 <= VMEM_BYTES
    return pltpu.CompilerParams(vmem_limit_bytes=vmem_mib * 1024 * 1024)


def _rms(x, g):
    ms = jnp.mean(x * x, axis=-1, keepdims=True)
    return x * lax.rsqrt(ms + EPS) * g


def _dot(a, b):
    return jnp.dot(a, b, preferred_element_type=F32)


def _dot_nt(a, b):
    return lax.dot_general(a, b, (((1,), (1,)), ((), ())), preferred_element_type=F32)


CAST_BLOCK_BYTES = 8 * 1024 * 1024


def _cast_body(w_ref, o_ref):
    o_ref[...] = w_ref[...].astype(BF16)


def _to_bf16(w):
    layers, k, n = w.shape
    tk = k
    while tk * n * 4 > CAST_BLOCK_BYTES and tk % (2 * BF16_ROWS) == 0:
        tk //= 2
    spec = pl.BlockSpec((None, tk, n), lambda l, i: (l, i, 0))
    return pl.pallas_call(
        _cast_body,
        grid=(layers, k // tk),
        in_specs=[spec],
        out_specs=spec,
        out_shape=jax.ShapeDtypeStruct(w.shape, BF16),
        compiler_params=_params(40),
        name="cast_bf16",
    )(w)


def _ffn_body(x_ref, g_ref, wg_ref, wu_ref, wd_ref, o_ref, h_ref, acc_ref):
    j = pl.program_id(1)

    @pl.when(j == 0)
    def _():
        h_ref[...] = _rms(x_ref[...], g_ref[...]).astype(BF16)
        acc_ref[...] = jnp.zeros_like(acc_ref)

    h = h_ref[...]
    gate = _dot(h, wg_ref[...])
    up = _dot(h, wu_ref[...])
    mid = (gate * jax.nn.sigmoid(gate) * up).astype(BF16)
    acc_ref[...] += _dot(mid, wd_ref[...])

    @pl.when(j == pl.num_programs(1) - 1)
    def _():
        o_ref[...] = x_ref[...] + 0.5 * acc_ref[...]


def _ffn(x, g, wg, wu, wd, layer, *, tm=512, tf=512):
    t, d = x.shape
    f = wg.shape[-1]
    tm, tf = min(tm, t), min(tf, f)
    return pl.pallas_call(
        _ffn_body,
        grid=(t // tm, f // tf),
        in_specs=[
            pl.BlockSpec((tm, d), lambda i, j: (i, 0)),
            pl.BlockSpec((1, d), lambda i, j: (0, 0)),
            pl.BlockSpec((None, d, tf), lambda i, j: (layer, 0, j)),
            pl.BlockSpec((None, d, tf), lambda i, j: (layer, 0, j)),
            pl.BlockSpec((None, tf, d), lambda i, j: (layer, j, 0)),
        ],
        out_specs=pl.BlockSpec((tm, d), lambda i, j: (i, 0)),
        out_shape=jax.ShapeDtypeStruct((t, d), F32),
        scratch_shapes=[pltpu.VMEM((tm, d), BF16), pltpu.VMEM((tm, d), F32)],
        compiler_params=_params(48),
        name="ffn",
    )(x, g, wg, wu, wd)


def _rope_tab_body(pos_ref, freq_ref, mc_ref, m1_ref, m2_ref, c_ref, s1_ref, s2_ref, ct_ref, st_ref):
    ang = pos_ref[...].astype(F32) * freq_ref[...]
    cos, sin = jnp.cos(ang), jnp.sin(ang)
    c_ref[...] = cos * mc_ref[...]
    s1_ref[...] = sin * m1_ref[...]
    s2_ref[...] = sin * m2_ref[...]
    ct_ref[...] = cos.T[:QK_ROPE // 2]
    st_ref[...] = sin.T[:QK_ROPE // 2]


def _rope_tables(positions, *, ts=1024):
    b, s = positions.shape
    ts = min(ts, s)
    half = QK_ROPE // 2
    inv_freq = 1.0 / (ROPE_THETA ** (jnp.arange(0, QK_ROPE, 2, dtype=F32) / QK_ROPE))
    zeros = jnp.zeros((LANES - QK_ROPE,), F32)
    ones, zhalf = jnp.ones((half,), F32), jnp.zeros((half,), F32)
    freq = jnp.concatenate([inv_freq, inv_freq, zeros])[None]
    mc = jnp.concatenate([ones, ones, zeros])[None]
    m1 = jnp.concatenate([-ones, zhalf, zeros])[None]
    m2 = jnp.concatenate([zhalf, ones, zeros])[None]
    row = pl.BlockSpec((1, LANES), lambda bi, i: (0, 0))
    tab = pl.BlockSpec((None, ts, LANES), lambda bi, i: (bi, i, 0))
    tab_t = pl.BlockSpec((None, half, ts), lambda bi, i: (bi, 0, i))
    shape = jax.ShapeDtypeStruct((b, s, LANES), F32)
    shape_t = jax.ShapeDtypeStruct((b, half, s), F32)
    return pl.pallas_call(
        _rope_tab_body,
        grid=(b, s // ts),
        in_specs=[pl.BlockSpec((None, ts, 1), lambda bi, i: (bi, i, 0)), row, row, row, row],
        out_specs=[tab, tab, tab, tab_t, tab_t],
        out_shape=[shape, shape, shape, shape_t, shape_t],
        name="rope_tables",
    )(positions[..., None], freq, mc, m1, m2)


def _rope(r, c, s1, s2):
    return r * c + pltpu.roll(r, LANES - QK_ROPE // 2, 1) * s1 + pltpu.roll(r, QK_ROPE // 2, 1) * s2


def _inproj_body(x_ref, g_ref, w_ref, gq_ref, gkv_ref, a_ref, cq_ref, ckv_ref, kpe_ref):
    h = _rms(x_ref[...], g_ref[...]).astype(BF16)
    cc = a_ref.shape[-1]
    o_q, o_kv, o_pe = 2 * cc, 2 * cc + Q_LORA, 2 * cc + Q_LORA + KV_LORA
    a_ref[...] = _dot(h, w_ref[:, :cc]) * jax.nn.sigmoid(_dot(h, w_ref[:, cc:o_q]))
    cq_ref[...] = _rms(_dot(h, w_ref[:, o_q:o_kv]), gq_ref[...]).astype(BF16)
    ckv_ref[...] = _rms(_dot(h, w_ref[:, o_kv:o_pe]), gkv_ref[...]).astype(BF16)
    kpe_ref[:, :QK_ROPE] = _dot(h, w_ref[:, o_pe:])
    kpe_ref[:, QK_ROPE:] = jnp.zeros((kpe_ref.shape[0], LANES - QK_ROPE), F32)


def _inproj(x, g, w, layer, gq, gkv, *, tm=512):
    b, s, d = x.shape
    tm = min(tm, s)
    cols = w.shape[-1]
    cc = (cols - Q_LORA - KV_LORA - QK_ROPE) // 2

    def const(shape):
        return pl.BlockSpec(shape, lambda bi, i: (0, 0))

    def tile(c):
        return pl.BlockSpec((None, tm, c), lambda bi, i: (bi, i, 0))

    return pl.pallas_call(
        _inproj_body,
        grid=(b, s // tm),
        in_specs=[tile(d), const((1, d)),
                  pl.BlockSpec((None, d, cols), lambda bi, i: (layer, 0, 0)),
                  const((1, Q_LORA)), const((1, KV_LORA))],
        out_specs=[tile(cc), tile(Q_LORA), tile(KV_LORA), tile(LANES)],
        out_shape=[jax.ShapeDtypeStruct((b, s, cc), F32),
                   jax.ShapeDtypeStruct((b, s, Q_LORA), BF16),
                   jax.ShapeDtypeStruct((b, s, KV_LORA), BF16),
                   jax.ShapeDtypeStruct((b, s, LANES), F32)],
        compiler_params=_params(56),
        name="inproj",
    )(x, g, w, gq, gkv)


LOG2E = 1.4426950408889634
VT_ROWS = V_HEAD + 16


def _qkv_body(cq_ref, ckv_ref, kpe_ref, c_ref, s1_ref, s2_ref, ct_ref, st_ref, wqt_ref, wk_ref,
              wvt_ref, gq_ref, gkn_ref, gkp_ref, q_ref, k_ref, vt_ref, qt_ref):
    scale = QK_HEAD ** -0.5 * LOG2E
    cq, ckv = cq_ref[...], ckv_ref[...]
    tm = cq.shape[0]
    half = QK_ROPE // 2

    qt_ref[...] = _dot_nt(wqt_ref[...], cq)
    ct, st = ct_ref[...], st_ref[...]
    gq = jnp.tile(gq_ref[...], (1, tm // LANES))
    for hd in range(MLA_HEADS):
        qh = qt_ref[hd * HEAD_PAD:(hd + 1) * HEAD_PAD, :]
        ss = jnp.sum(qh * qh, axis=0, keepdims=True)
        qn = qh * (lax.rsqrt(ss * (1.0 / QK_HEAD) + EPS) * scale) * gq
        x1, x2 = qn[QK_NOPE:QK_NOPE + half], qn[QK_NOPE + half:QK_HEAD]
        q_ref[hd, :QK_NOPE, :] = qn[:QK_NOPE].astype(BF16)
        q_ref[hd, QK_NOPE:QK_NOPE + half, :] = (x1 * ct - x2 * st).astype(BF16)
        q_ref[hd, QK_NOPE + half:QK_HEAD, :] = (x2 * ct + x1 * st).astype(BF16)
        q_ref[hd, QK_HEAD:, :] = jnp.zeros((HEAD_PAD - QK_HEAD, tm), BF16)

    kpe = kpe_ref[...]
    kpe_ss = jnp.sum(kpe * kpe, axis=-1, keepdims=True)
    kpe_rot = _rope(kpe * gkp_ref[...], c_ref[...], s1_ref[...], s2_ref[...])
    for hd in range(MLA_HEADS):
        kn = _dot(ckv, wk_ref[:, hd * QK_NOPE:(hd + 1) * QK_NOPE])
        ss = jnp.sum(kn * kn, axis=-1, keepdims=True) + kpe_ss
        rinv = lax.rsqrt(ss * (1.0 / QK_HEAD) + EPS)
        k_ref[hd, :, :LANES] = (kn * rinv * gkn_ref[...]).astype(BF16)
        k_ref[hd, :, LANES:] = (kpe_rot * rinv).astype(BF16)

        vt = _dot_nt(wvt_ref[hd * V_HEAD:(hd + 1) * V_HEAD, :], ckv)
        vt_ref[hd, :V_HEAD, :] = vt.astype(BF16)
        vt_ref[hd, V_HEAD:, :] = jnp.ones((VT_ROWS - V_HEAD, tm), BF16)


def _qkv(cq, ckv, kpe, tabs, wqt, wk, wvt, gq, gkn, gkp, *, tm):
    b, s, _ = cq.shape
    assert s % tm == 0
    half = QK_ROPE // 2

    def const(shape):
        return pl.BlockSpec(shape, lambda bi, i: (0, 0))

    def tile(c):
        return pl.BlockSpec((None, tm, c), lambda bi, i: (bi, i, 0))

    def heads(c):
        return pl.BlockSpec((None, MLA_HEADS, tm, c), lambda bi, i: (bi, 0, i, 0))

    return pl.pallas_call(
        _qkv_body,
        grid=(b, s // tm),
        in_specs=[tile(Q_LORA), tile(KV_LORA), tile(LANES), tile(LANES), tile(LANES), tile(LANES),
                  pl.BlockSpec((None, half, tm), lambda bi, i: (bi, 0, i)),
                  pl.BlockSpec((None, half, tm), lambda bi, i: (bi, 0, i)),
                  const(wqt.shape), const(wk.shape), const(wvt.shape),
                  const((HEAD_PAD, LANES)), const((1, LANES)), const((1, LANES))],
        out_specs=[pl.BlockSpec((None, MLA_HEADS, HEAD_PAD, tm), lambda bi, i: (bi, 0, 0, i)),
                   heads(HEAD_PAD),
                   pl.BlockSpec((None, MLA_HEADS, None, VT_ROWS, tm), lambda bi, i: (bi, 0, i, 0, 0))],
        out_shape=[jax.ShapeDtypeStruct((b, MLA_HEADS, HEAD_PAD, s), BF16),
                   jax.ShapeDtypeStruct((b, MLA_HEADS, s, HEAD_PAD), BF16),
                   jax.ShapeDtypeStruct((b, MLA_HEADS, s // tm, VT_ROWS, tm), BF16)],
        scratch_shapes=[pltpu.VMEM((MLA_HEADS * HEAD_PAD, tm), F32)],
        compiler_params=_params(56),
        name="qkv",
    )(cq, ckv, kpe, *tabs, wqt, wk, wvt, gq, gkn, gkp)


ATTN_QW = 256
ATTN_BLOCK = 1024
ATTN_AHEAD = 3


def _colmax(x):
    while x.shape[0] >= SUBLANES * SUBLANES:
        x = jnp.max(x.reshape(SUBLANES, x.shape[0] // SUBLANES, x.shape[1]), axis=0)
    return jnp.max(x, axis=0, keepdims=True)


def _attn_body(qt_ref, k_ref, vt_ref, o_ref, s_ref, m_ref, acc_ref, *, blk):
    qi = pl.program_id(2)
    chains = blk // ATTN_QW
    m_ref[...] = jnp.full(m_ref.shape, NEG_INF, F32)
    acc_ref[...] = jnp.zeros(acc_ref.shape, F32)

    def nkeys(c, diag):
        return (c + 1) * ATTN_QW if diag else blk

    def scores(ki, c, diag):
        n = nkeys(c, diag)
        start = pl.multiple_of(ki * blk, blk)
        s_ref[c, :n, :] = _dot(k_ref[pl.ds(start, n), :], qt_ref[:, c * ATTN_QW:(c + 1) * ATTN_QW])

    def softmax_pv(ki, c, diag):
        n = nkeys(c, diag)
        st = s_ref[c, :n, :]
        if diag:
            tail = st[n - ATTN_QW:]
            key = lax.broadcasted_iota(jnp.int32, tail.shape, 0) // CHUNK
            qry = lax.broadcasted_iota(jnp.int32, tail.shape, 1) // CHUNK
            tail = jnp.where(key <= qry, tail, NEG_INF)
            st = tail if n == ATTN_QW else jnp.concatenate([st[:n - ATTN_QW], tail], axis=0)
        m_old = m_ref[c]
        m_new = jnp.maximum(m_old, _colmax(st))
        p = jnp.exp2(st - m_new).astype(BF16)
        acc_ref[c] = jnp.exp2(m_old - m_new) * acc_ref[c] + _dot(vt_ref[ki, :, :n], p)
        m_ref[c] = m_new

    for c in range(ATTN_AHEAD):
        scores(0, c, False)

    def unmasked(ki, carry):
        for c in range(chains):
            ahead = c + ATTN_AHEAD
            scores(ki + ahead // chains, ahead % chains, False)
            softmax_pv(ki, c, False)
        return carry

    lax.fori_loop(0, qi, unmasked, 0)
    for c in range(chains):
        if c + ATTN_AHEAD < chains:
            scores(qi, c + ATTN_AHEAD, True)
        softmax_pv(qi, c, True)
    for c in range(chains):
        acc = acc_ref[c]
        out = acc[:V_HEAD] / acc[V_HEAD:V_HEAD + 1]
        o_ref[c * ATTN_QW:(c + 1) * ATTN_QW, :] = out.T.astype(BF16)


def _attention(qt, k, vt, *, blk):
    b, nh, s, _ = k.shape
    assert blk % CHUNK == 0 and blk % ATTN_QW == 0 and vt.shape[2:] == (s // blk, VT_ROWS, blk)
    chains = blk // ATTN_QW
    return pl.pallas_call(
        functools.partial(_attn_body, blk=blk),
        grid=(b, nh, s // blk),
        in_specs=[
            pl.BlockSpec((None, None, HEAD_PAD, blk), lambda bi, h, i: (bi, h, 0, i)),
            pl.BlockSpec((None, None, s, HEAD_PAD), lambda bi, h, i: (bi, h, 0, 0)),
            pl.BlockSpec((None, None, s // blk, VT_ROWS, blk), lambda bi, h, i: (bi, h, 0, 0, 0)),
        ],
        out_specs=pl.BlockSpec((None, blk, V_HEAD), lambda bi, h, i: (bi, i, h)),
        out_shape=jax.ShapeDtypeStruct((b, s, nh * V_HEAD), BF16),
        scratch_shapes=[pltpu.VMEM((chains, blk, ATTN_QW), F32),
                        pltpu.VMEM((chains, 1, ATTN_QW), F32),
                        pltpu.VMEM((chains, VT_ROWS, ATTN_QW), F32)],
        compiler_params=_params(40),
        name="mla_attention",
    )(qt, k, vt)


CONV_HALO = 32
CONV_PITCH = 4
CONV_ROWS = CONV_PITCH * SUBLANES
LN_ROWS = 256


def _conv_body(a_ref, prev_ref, w_ref, b_ref, g_ref, beta_ref, o_ref, buf_ref, y_ref):
    i = pl.program_id(1)
    tm, c = a_ref.shape
    tiles = c // LANES
    for lt in range(tiles):
        buf_ref[lt, CONV_HALO:, :] = a_ref[:, lt * LANES:(lt + 1) * LANES]

    @pl.when(i == 0)
    def _():
        buf_ref[:, :CONV_HALO, :] = jnp.zeros((tiles, CONV_HALO, LANES), F32)

    @pl.when(i > 0)
    def _():
        for lt in range(tiles):
            buf_ref[lt, :CONV_HALO, :] = prev_ref[:, lt * LANES:(lt + 1) * LANES]

    first = CONV_HALO - (CONV_WIDTH - 1)
    for lt in range(tiles):
        lanes = slice(lt * LANES, (lt + 1) * LANES)
        taps = [jnp.broadcast_to(w_ref[k:k + 1, lanes], (SUBLANES, LANES)) for k in range(CONV_WIDTH)]
        bias = jnp.broadcast_to(b_ref[:, lanes], (SUBLANES, LANES))

        def rows(r, carry, lt=lt, taps=taps, bias=bias):
            r0 = pl.multiple_of(r * CONV_ROWS, CONV_ROWS)
            acc = [[bias, None] for _ in range(CONV_PITCH)]
            for m in range(CONV_WIDTH + CONV_PITCH - 1):
                win = buf_ref[lt, pl.ds(r0 + first + m, SUBLANES, stride=CONV_PITCH), :]
                for j in range(CONV_PITCH):
                    if 0 <= m - j < CONV_WIDTH:
                        term = taps[m - j] * win
                        part = acc[j][m % 2]
                        acc[j][m % 2] = term if part is None else part + term
            for j in range(CONV_PITCH):
                y_ref[lt, pl.ds(r0 + j, SUBLANES, stride=CONV_PITCH), :] = acc[j][0] + acc[j][1]
            return carry

        lax.fori_loop(0, tm // CONV_ROWS, rows, 0)

    def norm(r, carry):
        r0 = pl.multiple_of(r * LN_ROWS, LN_ROWS)
        y = jnp.concatenate([y_ref[lt, pl.ds(r0, LN_ROWS), :] for lt in range(tiles)], axis=1)
        mu = jnp.mean(y, axis=-1, keepdims=True)
        yc = y - mu
        var = jnp.mean(yc * yc, axis=-1, keepdims=True)
        z = yc * lax.rsqrt(var + EPS) * g_ref[...] + beta_ref[...]
        o_ref[pl.ds(r0, LN_ROWS), :] = (z * jax.nn.sigmoid(z)).astype(BF16)
        return carry

    lax.fori_loop(0, tm // LN_ROWS, norm, 0)


def _conv(a, w, bias, g, beta, *, tm=512):
    b, s, c = a.shape
    tm = min(tm, s)
    assert tm % CONV_ROWS == 0 and tm % CONV_HALO == 0 and tm % LN_ROWS == 0 and c % LANES == 0
    per = tm // CONV_HALO
    tiles = c // LANES

    def const(shape):
        return pl.BlockSpec(shape, lambda bi, i: (0, 0))

    return pl.pallas_call(
        _conv_body,
        grid=(b, s // tm),
        in_specs=[
            pl.BlockSpec((None, tm, c), lambda bi, i: (bi, i, 0)),
            pl.BlockSpec((None, CONV_HALO, c), lambda bi, i: (bi, jnp.maximum(i * per - 1, 0), 0)),
            const(w.shape), const((1, c)), const((1, c)), const((1, c)),
        ],
        out_specs=pl.BlockSpec((None, tm, c), lambda bi, i: (bi, i, 0)),
        out_shape=jax.ShapeDtypeStruct((b, s, c), BF16),
        scratch_shapes=[pltpu.VMEM((tiles, CONV_HALO + tm, LANES), F32),
                        pltpu.VMEM((tiles, tm, LANES), F32)],
        compiler_params=_params(32),
        name="conv_module",
    )(a, a, w, bias, g, beta)


def _outproj_body(x_ref, yc_ref, ym_ref, wt_ref, wb_ref, o_ref):
    o_ref[...] = x_ref[...] + _dot(yc_ref[...], wt_ref[...]) + _dot(ym_ref[...], wb_ref[...])


def _outproj(x, yc, ym, w, layer, *, tm=512):
    b, s, d = x.shape
    tm = min(tm, s)
    cc = yc.shape[-1]
    assert w.shape[1] == 2 * cc

    def tile(c):
        return pl.BlockSpec((None, tm, c), lambda bi, i: (bi, i, 0))

    return pl.pallas_call(
        _outproj_body,
        grid=(b, s // tm),
        in_specs=[tile(d), tile(cc), tile(cc),
                  pl.BlockSpec((None, cc, d), lambda bi, i: (layer, 0, 0)),
                  pl.BlockSpec((None, cc, d), lambda bi, i: (layer, 1, 0))],
        out_specs=tile(d),
        out_shape=jax.ShapeDtypeStruct((b, s, d), F32),
        compiler_params=_params(48),
        name="outproj",
    )(x, yc, ym, w, w)


def _memkv_body(mem_ref, g_ref, wk_ref, wv_ref, gk_ref, k_ref, v_ref):
    m = _rms(mem_ref[...], g_ref[...]).astype(BF16)
    k_ref[...] = _rms(_dot(m, wk_ref[...].astype(BF16)), gk_ref[...]).astype(BF16)
    v_ref[...] = _dot(m, wv_ref[...].astype(BF16)).astype(BF16)


def _memkv(mem, g, wk, wv, layer, gk):
    rows, d = mem.shape
    hd = d // X_HEADS
    col = pl.BlockSpec((None, d, hd), lambda h: (layer, 0, h))
    out = pl.BlockSpec((rows, hd), lambda h: (0, h))
    shape = jax.ShapeDtypeStruct((rows, d), BF16)
    return pl.pallas_call(
        _memkv_body,
        grid=(X_HEADS,),
        in_specs=[pl.BlockSpec((rows, d), lambda h: (0, 0)), pl.BlockSpec((1, d), lambda h: (0, 0)),
                  col, col, pl.BlockSpec((1, hd), lambda h: (0, 0))],
        out_specs=[out, out],
        out_shape=[shape, shape],
        compiler_params=_params(40),
        name="mem_kv",
    )(mem, g, wk, wv, gk)


def _cross_body(x_ref, g_ref, wq_ref, gq_ref, k_ref, v_ref, wo_ref, o_ref, att_ref):
    x = x_ref[...]
    h = _rms(x, g_ref[...]).astype(BF16)
    hd = x.shape[-1] // X_HEADS
    scale = hd ** -0.5 * LOG2E
    cols = [slice(i * hd, (i + 1) * hd) for i in range(X_HEADS)]
    q = [_dot(h, wq_ref[:, c]) for c in cols]
    qn = [(qi * (lax.rsqrt(jnp.mean(qi * qi, axis=-1, keepdims=True) + EPS) * scale)
           * gq_ref[...]).astype(BF16) for qi in q]
    s = [_dot_nt(qi, k_ref[:, c]) for qi, c in zip(qn, cols)]
    p = [jnp.exp2(si - jnp.max(si, axis=-1, keepdims=True)) for si in s]
    for pi, c in zip(p, cols):
        l = jnp.sum(pi, axis=-1, keepdims=True)
        att_ref[:, c] = (_dot(pi.astype(BF16), v_ref[:, c]) / l).astype(BF16)
    o_ref[...] = x + _dot(att_ref[...], wo_ref[...])


def _cross(x, g, wq, gq, k, v, wo, layer, *, tm=512):
    b, s, d = x.shape
    tm = min(tm, s)
    m = k.shape[1]
    hd = d // X_HEADS

    def tile(c):
        return pl.BlockSpec((None, tm, c), lambda bi, i: (bi, i, 0))

    def const(shape):
        return pl.BlockSpec(shape, lambda bi, i: (0, 0))

    weight = pl.BlockSpec((None, d, d), lambda bi, i: (layer, 0, 0), pipeline_mode=pl.Buffered(1))

    mem = pl.BlockSpec((None, m, d), lambda bi, i: (bi, 0, 0))
    return pl.pallas_call(
        _cross_body,
        grid=(b, s // tm),
        in_specs=[tile(d), const((1, d)), weight, const((1, hd)), mem, mem, weight],
        out_specs=tile(d),
        out_shape=jax.ShapeDtypeStruct((b, s, d), F32),
        scratch_shapes=[pltpu.VMEM((tm, d), BF16)],
        compiler_params=_params(56),
        name="cross_attention",
    )(x, g, wq, gq, k, v, wo)


def _pad_cols(w, heads, width, padded):
    k = w.shape[0]
    w = w.reshape(k, heads, width)
    return jnp.pad(w, ((0, 0), (0, 0), (0, padded - width))).reshape(k, heads * padded)


def kernel(x, mem, positions, ffn1_norm, ffn1_w_gate, ffn1_w_up, ffn1_w_down, mix_norm, w_in, conv_w, conv_b, conv_ln_g, conv_ln_b, q_a_norm, w_q_b, kv_a_norm, w_kv_b, q_norm, k_norm, w_out, cross_norm, mem_norm, w_cq, w_ck, w_cv, cq_norm, ck_norm, w_co, ffn2_norm, ffn2_w_gate, ffn2_w_up, ffn2_w_down):
    b, s, d = x.shape
    n_mem = mem.shape[1]
    depth = w_in.shape[0]

    def row(v):
        return v.astype(F32)[None]

    tabs = _rope_tables(positions)
    mem2d = mem.reshape(b * n_mem, d)
    attn_blk = min(ATTN_BLOCK, s)
    ffn1 = [_to_bf16(w) for w in (ffn1_w_gate, ffn1_w_up, ffn1_w_down)]
    ffn2 = [_to_bf16(w) for w in (ffn2_w_gate, ffn2_w_up, ffn2_w_down)]
    w_out16, w_cq16, w_co16 = (_to_bf16(w) for w in (w_out, w_cq, w_co))
    w_in16 = w_in.astype(BF16)

    for l in range(depth):
        x = _ffn(x.reshape(b * s, d), row(ffn1_norm[l]), *ffn1, l).reshape(b, s, d)

        a, cq, ckv, kpe = _inproj(x, row(mix_norm[l]), w_in16, l, row(q_a_norm[l]), row(kv_a_norm[l]))

        wkv = w_kv_b[l].reshape(KV_LORA, MLA_HEADS, QK_NOPE + V_HEAD)
        wk = wkv[:, :, :QK_NOPE].reshape(KV_LORA, MLA_HEADS * QK_NOPE)
        wvt = wkv[:, :, QK_NOPE:].reshape(KV_LORA, MLA_HEADS * V_HEAD).T
        gq = jnp.broadcast_to(jnp.pad(q_norm[l], (0, HEAD_PAD - QK_HEAD))[:, None], (HEAD_PAD, LANES))
        gkp = jnp.pad(k_norm[l][QK_NOPE:], (0, LANES - QK_ROPE))
        q, k, vt = _qkv(cq, ckv, kpe, tabs,
                        _pad_cols(w_q_b[l], MLA_HEADS, QK_HEAD, HEAD_PAD).T.astype(BF16),
                        wk.astype(BF16), wvt.astype(BF16), gq.astype(F32), row(k_norm[l][:QK_NOPE]),
                        row(gkp), tm=attn_blk)
        y_mla = _attention(q, k, vt, blk=attn_blk)

        cw = jnp.pad(conv_w[l], ((0, CONV_HALO - CONV_WIDTH), (0, 0)))
        y_conv = _conv(a, cw, row(conv_b[l]), row(conv_ln_g[l]), row(conv_ln_b[l]))

        x = _outproj(x, y_conv, y_mla, w_out16, l)

        km, vm = _memkv(mem2d, row(mem_norm[l]), w_ck, w_cv, l, row(ck_norm[l]))
        x = _cross(x, row(cross_norm[l]), w_cq16, row(cq_norm[l]),
                   km.reshape(b, n_mem, d), vm.reshape(b, n_mem, d), w_co16, l)

        x = _ffn(x.reshape(b * s, d), row(ffn2_norm[l]), *ffn2, l).reshape(b, s, d)
    return x
```

```python
import functools

import jax
import jax.numpy as jnp
from jax import lax
from jax.experimental import pallas as pl
from jax.experimental.pallas import tpu as pltpu

CHUNK = 64
CONV_WIDTH = 31
MLA_HEADS = 8
QK_NOPE = 128
QK_ROPE = 64
QK_HEAD = QK_NOPE + QK_ROPE
V_HEAD = 128
Q_LORA = 768
KV_LORA = 256
ROPE_THETA = 10000.0
X_HEADS = 4
EPS = 1e-6
NEG_INF = -1e30

LANES = 128
SUBLANES = 8
BF16_ROWS = 16
VMEM_BYTES = 64 * 1024 * 1024
HEAD_PAD = 2 * LANES

F32 = jnp.float32
BF16 = jnp.bfloat16


def _params(vmem_mib):
    assert vmem_mib * 1024 * 1024 <= VMEM_BYTES
    return pltpu.CompilerParams(vmem_limit_bytes=vmem_mib * 1024 * 1024)


def _rms(x, g):
    ms = jnp.mean(x * x, axis=-1, keepdims=True)
    return x * lax.rsqrt(ms + EPS) * g


def _dot(a, b):
    return jnp.dot(a, b, preferred_element_type=F32)


def _dot_nt(a, b):
    return lax.dot_general(a, b, (((1,), (1,)), ((), ())), preferred_element_type=F32)


CAST_BLOCK_BYTES = 8 * 1024 * 1024


def _cast_body(w_ref, o_ref):
    o_ref[...] = w_ref[...].astype(BF16)


def _to_bf16(w):
    layers, k, n = w.shape
    tk = k
    while tk * n * 4 > CAST_BLOCK_BYTES and tk % (2 * BF16_ROWS) == 0:
        tk //= 2
    spec = pl.BlockSpec((None, tk, n), lambda l, i: (l, i, 0))
    return pl.pallas_call(
        _cast_body,
        grid=(layers, k // tk),
        in_specs=[spec],
        out_specs=spec,
        out_shape=jax.ShapeDtypeStruct(w.shape, BF16),
        compiler_params=_params(40),
        name="cast_bf16",
    )(w)


def _ffn_body(x_ref, g_ref, wg_ref, wu_ref, wd_ref, o_ref, h_ref):
    j = pl.program_id(1)

    @pl.when(j == 0)
    def _():
        h_ref[...] = _rms(x_ref[...], g_ref[...]).astype(BF16)
        o_ref[...] = jnp.zeros_like(o_ref)

    h = h_ref[...]
    gate = _dot(h, wg_ref[...])
    up = _dot(h, wu_ref[...])
    mid = (gate * jax.nn.sigmoid(gate) * up).astype(BF16)
    o_ref[...] += _dot(mid, wd_ref[...])

    @pl.when(j == pl.num_programs(1) - 1)
    def _():
        o_ref[...] = x_ref[...] + 0.5 * o_ref[...]


def _ffn(x, g, wg, wu, wd, layer, *, tm=1024, tf=512):
    t, d = x.shape
    f = wg.shape[-1]
    tm, tf = min(tm, t), min(tf, f)
    return pl.pallas_call(
        _ffn_body,
        grid=(t // tm, f // tf),
        in_specs=[
            pl.BlockSpec((tm, d), lambda i, j: (i, 0)),
            pl.BlockSpec((1, d), lambda i, j: (0, 0)),
            pl.BlockSpec((None, d, tf), lambda i, j: (layer, 0, j)),
            pl.BlockSpec((None, d, tf), lambda i, j: (layer, 0, j)),
            pl.BlockSpec((None, tf, d), lambda i, j: (layer, j, 0)),
        ],
        out_specs=pl.BlockSpec((tm, d), lambda i, j: (i, 0)),
        out_shape=jax.ShapeDtypeStruct((t, d), F32),
        scratch_shapes=[pltpu.VMEM((tm, d), BF16)],
        compiler_params=_params(60),
        name="ffn",
    )(x, g, wg, wu, wd)


def _rope_tab_body(pos_ref, freq_ref, mc_ref, m1_ref, m2_ref, c_ref, s1_ref, s2_ref, ct_ref, st_ref):
    ang = pos_ref[...].astype(F32) * freq_ref[...]
    cos, sin = jnp.cos(ang), jnp.sin(ang)
    c_ref[...] = cos * mc_ref[...]
    s1_ref[...] = sin * m1_ref[...]
    s2_ref[...] = sin * m2_ref[...]
    ct_ref[...] = cos.T[:QK_ROPE // 2]
    st_ref[...] = sin.T[:QK_ROPE // 2]


def _rope_tables(positions, *, ts=1024):
    b, s = positions.shape
    ts = min(ts, s)
    half = QK_ROPE // 2
    inv_freq = 1.0 / (ROPE_THETA ** (jnp.arange(0, QK_ROPE, 2, dtype=F32) / QK_ROPE))
    zeros = jnp.zeros((LANES - QK_ROPE,), F32)
    ones, zhalf = jnp.ones((half,), F32), jnp.zeros((half,), F32)
    freq = jnp.concatenate([inv_freq, inv_freq, zeros])[None]
    mc = jnp.concatenate([ones, ones, zeros])[None]
    m1 = jnp.concatenate([-ones, zhalf, zeros])[None]
    m2 = jnp.concatenate([zhalf, ones, zeros])[None]
    row = pl.BlockSpec((1, LANES), lambda bi, i: (0, 0))
    tab = pl.BlockSpec((None, ts, LANES), lambda bi, i: (bi, i, 0))
    tab_t = pl.BlockSpec((None, half, ts), lambda bi, i: (bi, 0, i))
    shape = jax.ShapeDtypeStruct((b, s, LANES), F32)
    shape_t = jax.ShapeDtypeStruct((b, half, s), F32)
    return pl.pallas_call(
        _rope_tab_body,
        grid=(b, s // ts),
        in_specs=[pl.BlockSpec((None, ts, 1), lambda bi, i: (bi, i, 0)), row, row, row, row],
        out_specs=[tab, tab, tab, tab_t, tab_t],
        out_shape=[shape, shape, shape, shape_t, shape_t],
        name="rope_tables",
    )(positions[..., None], freq, mc, m1, m2)


def _rope(r, c, s1, s2):
    return r * c + pltpu.roll(r, LANES - QK_ROPE // 2, 1) * s1 + pltpu.roll(r, QK_ROPE // 2, 1) * s2


def _inproj_body(x_ref, g_ref, w_ref, gq_ref, gkv_ref, a_ref, cq_ref, ckv_ref, kpe_ref):
    h = _rms(x_ref[...], g_ref[...]).astype(BF16)
    cc = a_ref.shape[-1]
    o_q, o_kv, o_pe = 2 * cc, 2 * cc + Q_LORA, 2 * cc + Q_LORA + KV_LORA
    a_ref[...] = _dot(h, w_ref[:, :cc]) * jax.nn.sigmoid(_dot(h, w_ref[:, cc:o_q]))
    cq_ref[...] = _rms(_dot(h, w_ref[:, o_q:o_kv]), gq_ref[...]).astype(BF16)
    ckv_ref[...] = _rms(_dot(h, w_ref[:, o_kv:o_pe]), gkv_ref[...]).astype(BF16)
    kpe_ref[:, :QK_ROPE] = _dot(h, w_ref[:, o_pe:])
    kpe_ref[:, QK_ROPE:] = jnp.zeros((kpe_ref.shape[0], LANES - QK_ROPE), F32)


def _inproj(x, g, w, layer, gq, gkv, *, tm=512):
    b, s, d = x.shape
    tm = min(tm, s)
    cols = w.shape[-1]
    cc = (cols - Q_LORA - KV_LORA - QK_ROPE) // 2

    def const(shape):
        return pl.BlockSpec(shape, lambda bi, i: (0, 0))

    def tile(c):
        return pl.BlockSpec((None, tm, c), lambda bi, i: (bi, i, 0))

    return pl.pallas_call(
        _inproj_body,
        grid=(b, s // tm),
        in_specs=[tile(d), const((1, d)),
                  pl.BlockSpec((None, d, cols), lambda bi, i: (layer, 0, 0)),
                  const((1, Q_LORA)), const((1, KV_LORA))],
        out_specs=[tile(cc), tile(Q_LORA), tile(KV_LORA), tile(LANES)],
        out_shape=[jax.ShapeDtypeStruct((b, s, cc), F32),
                   jax.ShapeDtypeStruct((b, s, Q_LORA), BF16),
                   jax.ShapeDtypeStruct((b, s, KV_LORA), BF16),
                   jax.ShapeDtypeStruct((b, s, LANES), F32)],
        compiler_params=_params(56),
        name="inproj",
    )(x, g, w, gq, gkv)


LOG2E = 1.4426950408889634
VT_ROWS = V_HEAD + 16


def _qkv_body(cq_ref, ckv_ref, kpe_ref, c_ref, s1_ref, s2_ref, ct_ref, st_ref, wqt_ref, wk_ref,
              wvt_ref, gq_ref, gkn_ref, gkp_ref, q_ref, k_ref, vt_ref, qt_ref):
    scale = QK_HEAD ** -0.5 * LOG2E
    cq, ckv = cq_ref[...], ckv_ref[...]
    tm = cq.shape[0]
    half = QK_ROPE // 2

    qt_ref[...] = _dot_nt(wqt_ref[...], cq)
    ct, st = ct_ref[...], st_ref[...]
    gq = jnp.tile(gq_ref[...], (1, tm // LANES))
    for hd in range(MLA_HEADS):
        qh = qt_ref[hd * HEAD_PAD:(hd + 1) * HEAD_PAD, :]
        ss = jnp.sum(qh * qh, axis=0, keepdims=True)
        qn = qh * (lax.rsqrt(ss * (1.0 / QK_HEAD) + EPS) * scale) * gq
        x1, x2 = qn[QK_NOPE:QK_NOPE + half], qn[QK_NOPE + half:QK_HEAD]
        q_ref[hd, :QK_NOPE, :] = qn[:QK_NOPE].astype(BF16)
        q_ref[hd, QK_NOPE:QK_NOPE + half, :] = (x1 * ct - x2 * st).astype(BF16)
        q_ref[hd, QK_NOPE + half:QK_HEAD, :] = (x2 * ct + x1 * st).astype(BF16)
        q_ref[hd, QK_HEAD:, :] = jnp.zeros((HEAD_PAD - QK_HEAD, tm), BF16)

    kpe = kpe_ref[...]
    kpe_ss = jnp.sum(kpe * kpe, axis=-1, keepdims=True)
    kpe_rot = _rope(kpe * gkp_ref[...], c_ref[...], s1_ref[...], s2_ref[...])
    for hd in range(MLA_HEADS):
        kn = _dot(ckv, wk_ref[:, hd * QK_NOPE:(hd + 1) * QK_NOPE])
        ss = jnp.sum(kn * kn, axis=-1, keepdims=True) + kpe_ss
        rinv = lax.rsqrt(ss * (1.0 / QK_HEAD) + EPS)
        k_ref[hd, :, :LANES] = (kn * rinv * gkn_ref[...]).astype(BF16)
        k_ref[hd, :, LANES:] = (kpe_rot * rinv).astype(BF16)

        vt = _dot_nt(wvt_ref[hd * V_HEAD:(hd + 1) * V_HEAD, :], ckv)
        vt_ref[hd, :V_HEAD, :] = vt.astype(BF16)
        vt_ref[hd, V_HEAD:, :] = jnp.ones((VT_ROWS - V_HEAD, tm), BF16)


def _qkv(cq, ckv, kpe, tabs, wqt, wk, wvt, gq, gkn, gkp, *, tm):
    b, s, _ = cq.shape
    assert s % tm == 0
    half = QK_ROPE // 2

    def const(shape):
        return pl.BlockSpec(shape, lambda bi, i: (0, 0))

    def tile(c):
        return pl.BlockSpec((None, tm, c), lambda bi, i: (bi, i, 0))

    def heads(c):
        return pl.BlockSpec((None, MLA_HEADS, tm, c), lambda bi, i: (bi, 0, i, 0))

    return pl.pallas_call(
        _qkv_body,
        grid=(b, s // tm),
        in_specs=[tile(Q_LORA), tile(KV_LORA), tile(LANES), tile(LANES), tile(LANES), tile(LANES),
                  pl.BlockSpec((None, half, tm), lambda bi, i: (bi, 0, i)),
                  pl.BlockSpec((None, half, tm), lambda bi, i: (bi, 0, i)),
                  const(wqt.shape), const(wk.shape), const(wvt.shape),
                  const((HEAD_PAD, LANES)), const((1, LANES)), const((1, LANES))],
        out_specs=[pl.BlockSpec((None, MLA_HEADS, HEAD_PAD, tm), lambda bi, i: (bi, 0, 0, i)),
                   heads(HEAD_PAD),
                   pl.BlockSpec((None, MLA_HEADS, None, VT_ROWS, tm), lambda bi, i: (bi, 0, i, 0, 0))],
        out_shape=[jax.ShapeDtypeStruct((b, MLA_HEADS, HEAD_PAD, s), BF16),
                   jax.ShapeDtypeStruct((b, MLA_HEADS, s, HEAD_PAD), BF16),
                   jax.ShapeDtypeStruct((b, MLA_HEADS, s // tm, VT_ROWS, tm), BF16)],
        scratch_shapes=[pltpu.VMEM((MLA_HEADS * HEAD_PAD, tm), F32)],
        compiler_params=_params(56),
        name="qkv",
    )(cq, ckv, kpe, *tabs, wqt, wk, wvt, gq, gkn, gkp)


ATTN_QW = 256
ATTN_BLOCK = 1024
ATTN_AHEAD = 3


def _colmax(x):
    while x.shape[0] >= SUBLANES * SUBLANES:
        x = jnp.max(x.reshape(SUBLANES, x.shape[0] // SUBLANES, x.shape[1]), axis=0)
    return jnp.max(x, axis=0, keepdims=True)


def _attn_body(qt_ref, k_ref, vt_ref, o_ref, s_ref, m_ref, acc_ref, *, blk):
    qi = pl.program_id(2)
    chains = blk // ATTN_QW
    m_ref[...] = jnp.full(m_ref.shape, NEG_INF, F32)
    acc_ref[...] = jnp.zeros(acc_ref.shape, F32)

    def nkeys(c, diag):
        return (c + 1) * ATTN_QW if diag else blk

    def scores(ki, c, diag):
        n = nkeys(c, diag)
        start = pl.multiple_of(ki * blk, blk)
        s_ref[c, :n, :] = _dot(k_ref[pl.ds(start, n), :], qt_ref[:, c * ATTN_QW:(c + 1) * ATTN_QW])

    def softmax_pv(ki, c, diag):
        n = nkeys(c, diag)
        st = s_ref[c, :n, :]
        if diag:
            tail = st[n - ATTN_QW:]
            key = lax.broadcasted_iota(jnp.int32, tail.shape, 0) // CHUNK
            qry = lax.broadcasted_iota(jnp.int32, tail.shape, 1) // CHUNK
            tail = jnp.where(key <= qry, tail, NEG_INF)
            st = tail if n == ATTN_QW else jnp.concatenate([st[:n - ATTN_QW], tail], axis=0)
        m_old = m_ref[c]
        m_new = jnp.maximum(m_old, _colmax(st))
        p = jnp.exp2(st - m_new).astype(BF16)
        acc_ref[c] = jnp.exp2(m_old - m_new) * acc_ref[c] + _dot(vt_ref[ki, :, :n], p)
        m_ref[c] = m_new

    for c in range(ATTN_AHEAD):
        scores(0, c, False)

    def unmasked(ki):
        for c in range(chains):
            ahead = c + ATTN_AHEAD
            scores(ki + ahead // chains, ahead % chains, False)
            softmax_pv(ki, c, False)

    def pair(t, carry):
        unmasked(2 * t)
        unmasked(2 * t + 1)
        return carry

    lax.fori_loop(0, qi // 2, pair, 0)

    @pl.when(qi % 2 == 1)
    def _():
        unmasked(qi - 1)

    for c in range(chains):
        if c + ATTN_AHEAD < chains:
            scores(qi, c + ATTN_AHEAD, True)
        softmax_pv(qi, c, True)
    for c in range(chains):
        acc = acc_ref[c]
        out = acc[:V_HEAD] / acc[V_HEAD:V_HEAD + 1]
        o_ref[c * ATTN_QW:(c + 1) * ATTN_QW, :] = out.T.astype(BF16)


def _attention(qt, k, vt, *, blk):
    b, nh, s, _ = k.shape
    assert blk % CHUNK == 0 and blk % ATTN_QW == 0 and vt.shape[2:] == (s // blk, VT_ROWS, blk)
    chains = blk // ATTN_QW
    return pl.pallas_call(
        functools.partial(_attn_body, blk=blk),
        grid=(b, nh, s // blk),
        in_specs=[
            pl.BlockSpec((None, None, HEAD_PAD, blk), lambda bi, h, i: (bi, h, 0, i)),
            pl.BlockSpec((None, None, s, HEAD_PAD), lambda bi, h, i: (bi, h, 0, 0)),
            pl.BlockSpec((None, None, s // blk, VT_ROWS, blk), lambda bi, h, i: (bi, h, 0, 0, 0)),
        ],
        out_specs=pl.BlockSpec((None, blk, V_HEAD), lambda bi, h, i: (bi, i, h)),
        out_shape=jax.ShapeDtypeStruct((b, s, nh * V_HEAD), BF16),
        scratch_shapes=[pltpu.VMEM((chains, blk, ATTN_QW), F32),
                        pltpu.VMEM((chains, 1, ATTN_QW), F32),
                        pltpu.VMEM((chains, VT_ROWS, ATTN_QW), F32)],
        compiler_params=_params(40),
        name="mla_attention",
    )(qt, k, vt)


CONV_HALO = 32
CONV_PITCH = 4
CONV_ROWS = CONV_PITCH * SUBLANES
LN_ROWS = 256


def _conv_body(a_ref, prev_ref, w_ref, b_ref, g_ref, beta_ref, o_ref, buf_ref, y_ref):
    i = pl.program_id(1)
    tm, c = a_ref.shape
    tiles = c // LANES
    for lt in range(tiles):
        buf_ref[lt, CONV_HALO:, :] = a_ref[:, lt * LANES:(lt + 1) * LANES]

    @pl.when(i == 0)
    def _():
        buf_ref[:, :CONV_HALO, :] = jnp.zeros((tiles, CONV_HALO, LANES), F32)

    @pl.when(i > 0)
    def _():
        for lt in range(tiles):
            buf_ref[lt, :CONV_HALO, :] = prev_ref[:, lt * LANES:(lt + 1) * LANES]

    first = CONV_HALO - (CONV_WIDTH - 1)
    for lt in range(tiles):
        lanes = slice(lt * LANES, (lt + 1) * LANES)
        taps = [jnp.broadcast_to(w_ref[k:k + 1, lanes], (SUBLANES, LANES)) for k in range(CONV_WIDTH)]
        bias = jnp.broadcast_to(b_ref[:, lanes], (SUBLANES, LANES))

        def rows(r, carry, lt=lt, taps=taps, bias=bias):
            r0 = pl.multiple_of(r * CONV_ROWS, CONV_ROWS)
            acc = [[bias, None] for _ in range(CONV_PITCH)]
            for m in range(CONV_WIDTH + CONV_PITCH - 1):
                win = buf_ref[lt, pl.ds(r0 + first + m, SUBLANES, stride=CONV_PITCH), :]
                for j in range(CONV_PITCH):
                    if 0 <= m - j < CONV_WIDTH:
                        term = taps[m - j] * win
                        part = acc[j][m % 2]
                        acc[j][m % 2] = term if part is None else part + term
            for j in range(CONV_PITCH):
                y_ref[lt, pl.ds(r0 + j, SUBLANES, stride=CONV_PITCH), :] = acc[j][0] + acc[j][1]
            return carry

        lax.fori_loop(0, tm // CONV_ROWS, rows, 0)

    def norm(r, carry):
        r0 = pl.multiple_of(r * LN_ROWS, LN_ROWS)
        y = jnp.concatenate([y_ref[lt, pl.ds(r0, LN_ROWS), :] for lt in range(tiles)], axis=1)
        mu = jnp.mean(y, axis=-1, keepdims=True)
        yc = y - mu
        var = jnp.mean(yc * yc, axis=-1, keepdims=True)
        z = yc * lax.rsqrt(var + EPS) * g_ref[...] + beta_ref[...]
        o_ref[pl.ds(r0, LN_ROWS), :] = (z * jax.nn.sigmoid(z)).astype(BF16)
        return carry

    lax.fori_loop(0, tm // LN_ROWS, norm, 0)


def _conv(a, w, bias, g, beta, *, tm=512):
    b, s, c = a.shape
    tm = min(tm, s)
    assert tm % CONV_ROWS == 0 and tm % CONV_HALO == 0 and tm % LN_ROWS == 0 and c % LANES == 0
    per = tm // CONV_HALO
    tiles = c // LANES

    def const(shape):
        return pl.BlockSpec(shape, lambda bi, i: (0, 0))

    return pl.pallas_call(
        _conv_body,
        grid=(b, s // tm),
        in_specs=[
            pl.BlockSpec((None, tm, c), lambda bi, i: (bi, i, 0)),
            pl.BlockSpec((None, CONV_HALO, c), lambda bi, i: (bi, jnp.maximum(i * per - 1, 0), 0)),
            const(w.shape), const((1, c)), const((1, c)), const((1, c)),
        ],
        out_specs=pl.BlockSpec((None, tm, c), lambda bi, i: (bi, i, 0)),
        out_shape=jax.ShapeDtypeStruct((b, s, c), BF16),
        scratch_shapes=[pltpu.VMEM((tiles, CONV_HALO + tm, LANES), F32),
                        pltpu.VMEM((tiles, tm, LANES), F32)],
        compiler_params=_params(32),
        name="conv_module",
    )(a, a, w, bias, g, beta)


def _outproj_body(x_ref, yc_ref, ym_ref, wt_ref, wb_ref, o_ref):
    o_ref[...] = x_ref[...] + _dot(yc_ref[...], wt_ref[...]) + _dot(ym_ref[...], wb_ref[...])


def _outproj(x, yc, ym, w, layer, *, tm=512):
    b, s, d = x.shape
    tm = min(tm, s)
    cc = yc.shape[-1]
    assert w.shape[1] == 2 * cc

    def tile(c):
        return pl.BlockSpec((None, tm, c), lambda bi, i: (bi, i, 0))

    return pl.pallas_call(
        _outproj_body,
        grid=(b, s // tm),
        in_specs=[tile(d), tile(cc), tile(cc),
                  pl.BlockSpec((None, cc, d), lambda bi, i: (layer, 0, 0)),
                  pl.BlockSpec((None, cc, d), lambda bi, i: (layer, 1, 0))],
        out_specs=tile(d),
        out_shape=jax.ShapeDtypeStruct((b, s, d), F32),
        compiler_params=_params(48),
        name="outproj",
    )(x, yc, ym, w, w)


def _memkv_body(mem_ref, g_ref, wk_ref, wv_ref, gk_ref, k_ref, v_ref):
    m = _rms(mem_ref[...], g_ref[...]).astype(BF16)
    k_ref[...] = _rms(_dot(m, wk_ref[...].astype(BF16)), gk_ref[...]).astype(BF16)
    v_ref[...] = _dot(m, wv_ref[...].astype(BF16)).astype(BF16)


def _memkv(mem, g, wk, wv, layer, gk):
    rows, d = mem.shape
    hd = d // X_HEADS
    col = pl.BlockSpec((None, d, hd), lambda h: (layer, 0, h))
    out = pl.BlockSpec((rows, hd), lambda h: (0, h))
    shape = jax.ShapeDtypeStruct((rows, d), BF16)
    return pl.pallas_call(
        _memkv_body,
        grid=(X_HEADS,),
        in_specs=[pl.BlockSpec((rows, d), lambda h: (0, 0)), pl.BlockSpec((1, d), lambda h: (0, 0)),
                  col, col, pl.BlockSpec((1, hd), lambda h: (0, 0))],
        out_specs=[out, out],
        out_shape=[shape, shape],
        compiler_params=_params(40),
        name="mem_kv",
    )(mem, g, wk, wv, gk)


def _cross_body(x_ref, g_ref, wq_ref, gq_ref, k_ref, v_ref, wo_ref, o_ref, att_ref):
    x = x_ref[...]
    h = _rms(x, g_ref[...]).astype(BF16)
    hd = x.shape[-1] // X_HEADS
    scale = hd ** -0.5 * LOG2E
    cols = [slice(i * hd, (i + 1) * hd) for i in range(X_HEADS)]
    q = [_dot(h, wq_ref[:, c]) for c in cols]
    qn = [(qi * (lax.rsqrt(jnp.mean(qi * qi, axis=-1, keepdims=True) + EPS) * scale)
           * gq_ref[...]).astype(BF16) for qi in q]
    s = [_dot_nt(qi, k_ref[:, c]) for qi, c in zip(qn, cols)]
    p = [jnp.exp2(si - jnp.max(si, axis=-1, keepdims=True)) for si in s]
    for pi, c in zip(p, cols):
        l = jnp.sum(pi, axis=-1, keepdims=True)
        att_ref[:, c] = (_dot(pi.astype(BF16), v_ref[:, c]) / l).astype(BF16)
    o_ref[...] = x + _dot(att_ref[...], wo_ref[...])


def _cross(x, g, wq, gq, k, v, wo, layer, *, tm=512):
    b, s, d = x.shape
    tm = min(tm, s)
    m = k.shape[1]
    hd = d // X_HEADS

    def tile(c):
        return pl.BlockSpec((None, tm, c), lambda bi, i: (bi, i, 0))

    def const(shape):
        return pl.BlockSpec(shape, lambda bi, i: (0, 0))

    weight = pl.BlockSpec((None, d, d), lambda bi, i: (layer, 0, 0), pipeline_mode=pl.Buffered(1))

    mem = pl.BlockSpec((None, m, d), lambda bi, i: (bi, 0, 0))
    return pl.pallas_call(
        _cross_body,
        grid=(b, s // tm),
        in_specs=[tile(d), const((1, d)), weight, const((1, hd)), mem, mem, weight],
        out_specs=tile(d),
        out_shape=jax.ShapeDtypeStruct((b, s, d), F32),
        scratch_shapes=[pltpu.VMEM((tm, d), BF16)],
        compiler_params=_params(56),
        name="cross_attention",
    )(x, g, wq, gq, k, v, wo)


def _pad_cols(w, heads, width, padded):
    k = w.shape[0]
    w = w.reshape(k, heads, width)
    return jnp.pad(w, ((0, 0), (0, 0), (0, padded - width))).reshape(k, heads * padded)


def kernel(x, mem, positions, ffn1_norm, ffn1_w_gate, ffn1_w_up, ffn1_w_down, mix_norm, w_in, conv_w, conv_b, conv_ln_g, conv_ln_b, q_a_norm, w_q_b, kv_a_norm, w_kv_b, q_norm, k_norm, w_out, cross_norm, mem_norm, w_cq, w_ck, w_cv, cq_norm, ck_norm, w_co, ffn2_norm, ffn2_w_gate, ffn2_w_up, ffn2_w_down):
    b, s, d = x.shape
    n_mem = mem.shape[1]
    depth = w_in.shape[0]

    def row(v):
        return v.astype(F32)[None]

    tabs = _rope_tables(positions)
    mem2d = mem.reshape(b * n_mem, d)
    attn_blk = min(ATTN_BLOCK, s)
    ffn1 = [_to_bf16(w) for w in (ffn1_w_gate, ffn1_w_up, ffn1_w_down)]
    ffn2 = [_to_bf16(w) for w in (ffn2_w_gate, ffn2_w_up, ffn2_w_down)]
    w_out16, w_cq16, w_co16 = (_to_bf16(w) for w in (w_out, w_cq, w_co))
    w_in16 = w_in.astype(BF16)

    for l in range(depth):
        x = _ffn(x.reshape(b * s, d), row(ffn1_norm[l]), *ffn1, l).reshape(b, s, d)

        a, cq, ckv, kpe = _inproj(x, row(mix_norm[l]), w_in16, l, row(q_a_norm[l]), row(kv_a_norm[l]))

        wkv = w_kv_b[l].reshape(KV_LORA, MLA_HEADS, QK_NOPE + V_HEAD)
        wk = wkv[:, :, :QK_NOPE].reshape(KV_LORA, MLA_HEADS * QK_NOPE)
        wvt = wkv[:, :, QK_NOPE:].reshape(KV_LORA, MLA_HEADS * V_HEAD).T
        gq = jnp.broadcast_to(jnp.pad(q_norm[l], (0, HEAD_PAD - QK_HEAD))[:, None], (HEAD_PAD, LANES))
        gkp = jnp.pad(k_norm[l][QK_NOPE:], (0, LANES - QK_ROPE))
        q, k, vt = _qkv(cq, ckv, kpe, tabs,
                        _pad_cols(w_q_b[l], MLA_HEADS, QK_HEAD, HEAD_PAD).T.astype(BF16),
                        wk.astype(BF16), wvt.astype(BF16), gq.astype(F32), row(k_norm[l][:QK_NOPE]),
                        row(gkp), tm=attn_blk)
        y_mla = _attention(q, k, vt, blk=attn_blk)

        cw = jnp.pad(conv_w[l], ((0, CONV_HALO - CONV_WIDTH), (0, 0)))
        y_conv = _conv(a, cw, row(conv_b[l]), row(conv_ln_g[l]), row(conv_ln_b[l]))

        x = _outproj(x, y_conv, y_mla, w_out16, l)

        km, vm = _memkv(mem2d, row(mem_norm[l]), w_ck, w_cv, l, row(ck_norm[l]))
        x = _cross(x, row(cross_norm[l]), w_cq16, row(cq_norm[l]),
                   km.reshape(b, n_mem, d), vm.reshape(b, n_mem, d), w_co16, l)

        x = _ffn(x.reshape(b * s, d), row(ffn2_norm[l]), *ffn2, l).reshape(b, s, d)
    return x
```

```python
import functools

import jax
import jax.numpy as jnp
from jax import lax
from jax.experimental import pallas as pl
from jax.experimental.pallas import tpu as pltpu

CHUNK = 64
CONV_WIDTH = 31
MLA_HEADS = 8
QK_NOPE = 128
QK_ROPE = 64
QK_HEAD = QK_NOPE + QK_ROPE
V_HEAD = 128
Q_LORA = 768
KV_LORA = 256
ROPE_THETA = 10000.0
X_HEADS = 4
EPS = 1e-6
NEG_INF = -1e30

LANES = 128
SUBLANES = 8
BF16_ROWS = 16
VMEM_BYTES = 64 * 1024 * 1024
HEAD_PAD = 2 * LANES

F32 = jnp.float32
BF16 = jnp.bfloat16


def _params(vmem_mib):
    assert vmem_mib * 1024 * 1024 <= VMEM_BYTES
    return pltpu.CompilerParams(vmem_limit_bytes=vmem_mib * 1024 * 1024)


def _rms(x, g):
    ms = jnp.mean(x * x, axis=-1, keepdims=True)
    return x * lax.rsqrt(ms + EPS) * g


def _dot(a, b):
    return jnp.dot(a, b, preferred_element_type=F32)


def _dot_nt(a, b):
    return lax.dot_general(a, b, (((1,), (1,)), ((), ())), preferred_element_type=F32)


CAST_BLOCK_BYTES = 8 * 1024 * 1024


def _cast_body(w_ref, o_ref):
    o_ref[...] = w_ref[...].astype(BF16)


def _to_bf16(w):
    layers, k, n = w.shape
    tk = k
    while tk * n * 4 > CAST_BLOCK_BYTES and tk % (2 * BF16_ROWS) == 0:
        tk //= 2
    spec = pl.BlockSpec((None, tk, n), lambda l, i: (l, i, 0))
    return pl.pallas_call(
        _cast_body,
        grid=(layers, k // tk),
        in_specs=[spec],
        out_specs=spec,
        out_shape=jax.ShapeDtypeStruct(w.shape, BF16),
        compiler_params=_params(40),
        name="cast_bf16",
    )(w)


def _ffn_body(x_ref, g_ref, wg_ref, wu_ref, wd_ref, o_ref, h_ref):
    j = pl.program_id(1)

    @pl.when(j == 0)
    def _():
        x = x_ref[...]
        h_ref[...] = _rms(x, g_ref[...]).astype(BF16)
        o_ref[...] = x

    h = h_ref[...]
    gate = _dot(h, wg_ref[...])
    up = _dot(h, wu_ref[...])
    mid = (gate * jax.nn.sigmoid(gate) * (0.5 * up)).astype(BF16)
    o_ref[...] += _dot(mid, wd_ref[...])


def _ffn(x, g, wg, wu, wd, layer, *, tm=1024, tf=512):
    t, d = x.shape
    f = wg.shape[-1]
    tm, tf = min(tm, t), min(tf, f)
    return pl.pallas_call(
        _ffn_body,
        grid=(t // tm, f // tf),
        in_specs=[
            pl.BlockSpec((tm, d), lambda i, j: (i, 0)),
            pl.BlockSpec((1, d), lambda i, j: (0, 0)),
            pl.BlockSpec((None, d, tf), lambda i, j: (layer, 0, j)),
            pl.BlockSpec((None, d, tf), lambda i, j: (layer, 0, j)),
            pl.BlockSpec((None, tf, d), lambda i, j: (layer, j, 0)),
        ],
        out_specs=pl.BlockSpec((tm, d), lambda i, j: (i, 0)),
        out_shape=jax.ShapeDtypeStruct((t, d), F32),
        scratch_shapes=[pltpu.VMEM((tm, d), BF16)],
        compiler_params=_params(60),
        name="ffn",
    )(x, g, wg, wu, wd)


def _rope_tab_body(pos_ref, freq_ref, mc_ref, m1_ref, m2_ref, c_ref, s1_ref, s2_ref, ct_ref, st_ref):
    ang = pos_ref[...].astype(F32) * freq_ref[...]
    cos, sin = jnp.cos(ang), jnp.sin(ang)
    c_ref[...] = cos * mc_ref[...]
    s1_ref[...] = sin * m1_ref[...]
    s2_ref[...] = sin * m2_ref[...]
    ct_ref[...] = cos.T[:QK_ROPE // 2]
    st_ref[...] = sin.T[:QK_ROPE // 2]


def _rope_tables(positions, *, ts=1024):
    b, s = positions.shape
    ts = min(ts, s)
    half = QK_ROPE // 2
    inv_freq = 1.0 / (ROPE_THETA ** (jnp.arange(0, QK_ROPE, 2, dtype=F32) / QK_ROPE))
    zeros = jnp.zeros((LANES - QK_ROPE,), F32)
    ones, zhalf = jnp.ones((half,), F32), jnp.zeros((half,), F32)
    freq = jnp.concatenate([inv_freq, inv_freq, zeros])[None]
    mc = jnp.concatenate([ones, ones, zeros])[None]
    m1 = jnp.concatenate([-ones, zhalf, zeros])[None]
    m2 = jnp.concatenate([zhalf, ones, zeros])[None]
    row = pl.BlockSpec((1, LANES), lambda bi, i: (0, 0))
    tab = pl.BlockSpec((None, ts, LANES), lambda bi, i: (bi, i, 0))
    tab_t = pl.BlockSpec((None, half, ts), lambda bi, i: (bi, 0, i))
    shape = jax.ShapeDtypeStruct((b, s, LANES), F32)
    shape_t = jax.ShapeDtypeStruct((b, half, s), F32)
    return pl.pallas_call(
        _rope_tab_body,
        grid=(b, s // ts),
        in_specs=[pl.BlockSpec((None, ts, 1), lambda bi, i: (bi, i, 0)), row, row, row, row],
        out_specs=[tab, tab, tab, tab_t, tab_t],
        out_shape=[shape, shape, shape, shape_t, shape_t],
        name="rope_tables",
    )(positions[..., None], freq, mc, m1, m2)


def _rope(r, c, s1, s2):
    return r * c + pltpu.roll(r, LANES - QK_ROPE // 2, 1) * s1 + pltpu.roll(r, QK_ROPE // 2, 1) * s2


def _inproj_body(x_ref, g_ref, w_ref, gq_ref, gkv_ref, a_ref, cq_ref, ckv_ref, kpe_ref):
    h = _rms(x_ref[...], g_ref[...]).astype(BF16)
    cc = a_ref.shape[-1]
    o_q, o_kv, o_pe = 2 * cc, 2 * cc + Q_LORA, 2 * cc + Q_LORA + KV_LORA
    a_ref[...] = _dot(h, w_ref[:, :cc]) * jax.nn.sigmoid(_dot(h, w_ref[:, cc:o_q]))
    cq_ref[...] = _rms(_dot(h, w_ref[:, o_q:o_kv]), gq_ref[...]).astype(BF16)
    ckv_ref[...] = _rms(_dot(h, w_ref[:, o_kv:o_pe]), gkv_ref[...]).astype(BF16)
    kpe_ref[:, :QK_ROPE] = _dot(h, w_ref[:, o_pe:])
    kpe_ref[:, QK_ROPE:] = jnp.zeros((kpe_ref.shape[0], LANES - QK_ROPE), F32)


def _inproj(x, g, w, layer, gq, gkv, *, tm=512):
    b, s, d = x.shape
    tm = min(tm, s)
    cols = w.shape[-1]
    cc = (cols - Q_LORA - KV_LORA - QK_ROPE) // 2

    def const(shape):
        return pl.BlockSpec(shape, lambda bi, i: (0, 0))

    def tile(c):
        return pl.BlockSpec((None, tm, c), lambda bi, i: (bi, i, 0))

    return pl.pallas_call(
        _inproj_body,
        grid=(b, s // tm),
        in_specs=[tile(d), const((1, d)),
                  pl.BlockSpec((None, d, cols), lambda bi, i: (layer, 0, 0)),
                  const((1, Q_LORA)), const((1, KV_LORA))],
        out_specs=[tile(cc), tile(Q_LORA), tile(KV_LORA), tile(LANES)],
        out_shape=[jax.ShapeDtypeStruct((b, s, cc), F32),
                   jax.ShapeDtypeStruct((b, s, Q_LORA), BF16),
                   jax.ShapeDtypeStruct((b, s, KV_LORA), BF16),
                   jax.ShapeDtypeStruct((b, s, LANES), F32)],
        compiler_params=_params(56),
        name="inproj",
    )(x, g, w, gq, gkv)


LOG2E = 1.4426950408889634
VT_ROWS = V_HEAD + 16


def _qkv_body(cq_ref, ckv_ref, kpe_ref, c_ref, s1_ref, s2_ref, ct_ref, st_ref, wqt_ref, wk_ref,
              wvt_ref, gq_ref, gkn_ref, gkp_ref, q_ref, k_ref, vt_ref, qt_ref):
    scale = QK_HEAD ** -0.5 * LOG2E
    cq, ckv = cq_ref[...], ckv_ref[...]
    tm = cq.shape[0]
    half = QK_ROPE // 2

    qt_ref[...] = _dot_nt(wqt_ref[...], cq)
    ct, st = ct_ref[...], st_ref[...]
    gq = jnp.tile(gq_ref[...], (1, tm // LANES))
    for hd in range(MLA_HEADS):
        qh = qt_ref[hd * HEAD_PAD:(hd + 1) * HEAD_PAD, :]
        ss = jnp.sum(qh * qh, axis=0, keepdims=True)
        qn = qh * (lax.rsqrt(ss * (1.0 / QK_HEAD) + EPS) * scale) * gq
        x1, x2 = qn[QK_NOPE:QK_NOPE + half], qn[QK_NOPE + half:QK_HEAD]
        q_ref[hd, :QK_NOPE, :] = qn[:QK_NOPE].astype(BF16)
        q_ref[hd, QK_NOPE:QK_NOPE + half, :] = (x1 * ct - x2 * st).astype(BF16)
        q_ref[hd, QK_NOPE + half:QK_HEAD, :] = (x2 * ct + x1 * st).astype(BF16)
        q_ref[hd, QK_HEAD:, :] = jnp.zeros((HEAD_PAD - QK_HEAD, tm), BF16)

    kpe = kpe_ref[...]
    kpe_ss = jnp.sum(kpe * kpe, axis=-1, keepdims=True)
    kpe_rot = _rope(kpe * gkp_ref[...], c_ref[...], s1_ref[...], s2_ref[...])
    for hd in range(MLA_HEADS):
        kn = _dot(ckv, wk_ref[:, hd * QK_NOPE:(hd + 1) * QK_NOPE])
        ss = jnp.sum(kn * kn, axis=-1, keepdims=True) + kpe_ss
        rinv = lax.rsqrt(ss * (1.0 / QK_HEAD) + EPS)
        k_ref[hd, :, :LANES] = (kn * rinv * gkn_ref[...]).astype(BF16)
        k_ref[hd, :, LANES:] = (kpe_rot * rinv).astype(BF16)

        vt = _dot_nt(wvt_ref[hd * V_HEAD:(hd + 1) * V_HEAD, :], ckv)
        vt_ref[hd, :V_HEAD, :] = vt.astype(BF16)
        vt_ref[hd, V_HEAD:, :] = jnp.ones((VT_ROWS - V_HEAD, tm), BF16)


def _qkv(cq, ckv, kpe, tabs, wqt, wk, wvt, gq, gkn, gkp, *, tm):
    b, s, _ = cq.shape
    assert s % tm == 0
    half = QK_ROPE // 2

    def const(shape):
        return pl.BlockSpec(shape, lambda bi, i: (0, 0))

    def tile(c):
        return pl.BlockSpec((None, tm, c), lambda bi, i: (bi, i, 0))

    def heads(c):
        return pl.BlockSpec((None, MLA_HEADS, tm, c), lambda bi, i: (bi, 0, i, 0))

    return pl.pallas_call(
        _qkv_body,
        grid=(b, s // tm),
        in_specs=[tile(Q_LORA), tile(KV_LORA), tile(LANES), tile(LANES), tile(LANES), tile(LANES),
                  pl.BlockSpec((None, half, tm), lambda bi, i: (bi, 0, i)),
                  pl.BlockSpec((None, half, tm), lambda bi, i: (bi, 0, i)),
                  const(wqt.shape), const(wk.shape), const(wvt.shape),
                  const((HEAD_PAD, LANES)), const((1, LANES)), const((1, LANES))],
        out_specs=[pl.BlockSpec((None, MLA_HEADS, HEAD_PAD, tm), lambda bi, i: (bi, 0, 0, i)),
                   heads(HEAD_PAD),
                   pl.BlockSpec((None, MLA_HEADS, None, VT_ROWS, tm), lambda bi, i: (bi, 0, i, 0, 0))],
        out_shape=[jax.ShapeDtypeStruct((b, MLA_HEADS, HEAD_PAD, s), BF16),
                   jax.ShapeDtypeStruct((b, MLA_HEADS, s, HEAD_PAD), BF16),
                   jax.ShapeDtypeStruct((b, MLA_HEADS, s // tm, VT_ROWS, tm), BF16)],
        scratch_shapes=[pltpu.VMEM((MLA_HEADS * HEAD_PAD, tm), F32)],
        compiler_params=_params(56),
        name="qkv",
    )(cq, ckv, kpe, *tabs, wqt, wk, wvt, gq, gkn, gkp)


ATTN_QW = 256
ATTN_BLOCK = 1024
ATTN_AHEAD = 3


def _colmax(x):
    while x.shape[0] >= SUBLANES * SUBLANES:
        x = jnp.max(x.reshape(SUBLANES, x.shape[0] // SUBLANES, x.shape[1]), axis=0)
    return jnp.max(x, axis=0, keepdims=True)


def _attn_body(qt_ref, k_ref, vt_ref, o_ref, s_ref, mx_ref, m_ref, acc_ref, *, blk):
    qi = pl.program_id(2)
    chains = blk // ATTN_QW
    m_ref[...] = jnp.full(m_ref.shape, NEG_INF, F32)
    acc_ref[...] = jnp.zeros(acc_ref.shape, F32)

    def nkeys(c, diag):
        return (c + 1) * ATTN_QW if diag else blk

    def scores(ki, c, diag):
        n = nkeys(c, diag)
        start = pl.multiple_of(ki * blk, blk)
        st = _dot(k_ref[pl.ds(start, n), :], qt_ref[:, c * ATTN_QW:(c + 1) * ATTN_QW])
        s_ref[c, :n, :] = st
        if not diag:
            mx_ref[c] = _colmax(st)

    def softmax_pv(ki, c, diag):
        n = nkeys(c, diag)
        st = s_ref[c, :n, :]
        if diag:
            tail = st[n - ATTN_QW:]
            key = lax.broadcasted_iota(jnp.int32, tail.shape, 0) // CHUNK
            qry = lax.broadcasted_iota(jnp.int32, tail.shape, 1) // CHUNK
            tail = jnp.where(key <= qry, tail, NEG_INF)
            st = tail if n == ATTN_QW else jnp.concatenate([st[:n - ATTN_QW], tail], axis=0)
        m_old = m_ref[c]
        m_new = jnp.maximum(m_old, _colmax(st) if diag else mx_ref[c])
        p = jnp.exp2(st - m_new).astype(BF16)
        acc_ref[c] = jnp.exp2(m_old - m_new) * acc_ref[c] + _dot(vt_ref[ki, :, :n], p)
        m_ref[c] = m_new

    def finalize(c):
        acc = acc_ref[c]
        out = acc[:V_HEAD] / acc[V_HEAD:V_HEAD + 1]
        o_ref[c * ATTN_QW:(c + 1) * ATTN_QW, :] = out.T.astype(BF16)

    for c in range(ATTN_AHEAD):
        scores(0, c, False)

    def unmasked(ki):
        for c in range(chains):
            ahead = c + ATTN_AHEAD
            scores(ki + ahead // chains, ahead % chains, False)
            softmax_pv(ki, c, False)

    def pair(t, carry):
        unmasked(2 * t)
        unmasked(2 * t + 1)
        return carry

    lax.fori_loop(0, qi // 2, pair, 0)

    @pl.when(qi % 2 == 1)
    def _():
        unmasked(qi - 1)

    for c in range(chains):
        if c + ATTN_AHEAD < chains:
            scores(qi, c + ATTN_AHEAD, True)
        softmax_pv(qi, c, True)
    for c in range(chains):
        finalize(c)


def _attention(qt, k, vt, *, blk):
    b, nh, s, _ = k.shape
    assert blk % CHUNK == 0 and blk % ATTN_QW == 0 and vt.shape[2:] == (s // blk, VT_ROWS, blk)
    chains = blk // ATTN_QW
    return pl.pallas_call(
        functools.partial(_attn_body, blk=blk),
        grid=(b, nh, s // blk),
        in_specs=[
            pl.BlockSpec((None, None, HEAD_PAD, blk), lambda bi, h, i: (bi, h, 0, i)),
            pl.BlockSpec((None, None, s, HEAD_PAD), lambda bi, h, i: (bi, h, 0, 0)),
            pl.BlockSpec((None, None, s // blk, VT_ROWS, blk), lambda bi, h, i: (bi, h, 0, 0, 0)),
        ],
        out_specs=pl.BlockSpec((None, blk, V_HEAD), lambda bi, h, i: (bi, i, h)),
        out_shape=jax.ShapeDtypeStruct((b, s, nh * V_HEAD), BF16),
        scratch_shapes=[pltpu.VMEM((chains, blk, ATTN_QW), F32),
                        pltpu.VMEM((chains, 1, ATTN_QW), F32),
                        pltpu.VMEM((chains, 1, ATTN_QW), F32),
                        pltpu.VMEM((chains, VT_ROWS, ATTN_QW), F32)],
        compiler_params=_params(40),
        name="mla_attention",
    )(qt, k, vt)


CONV_HALO = 32
CONV_PITCH = 4
CONV_ROWS = CONV_PITCH * SUBLANES
LN_ROWS = 256


def _conv_body(a_ref, prev_ref, w_ref, b_ref, g_ref, beta_ref, o_ref, buf_ref, y_ref):
    i = pl.program_id(1)
    tm, c = a_ref.shape
    tiles = c // LANES
    for lt in range(tiles):
        buf_ref[lt, CONV_HALO:, :] = a_ref[:, lt * LANES:(lt + 1) * LANES]

    @pl.when(i == 0)
    def _():
        buf_ref[:, :CONV_HALO, :] = jnp.zeros((tiles, CONV_HALO, LANES), F32)

    @pl.when(i > 0)
    def _():
        for lt in range(tiles):
            buf_ref[lt, :CONV_HALO, :] = prev_ref[:, lt * LANES:(lt + 1) * LANES]

    first = CONV_HALO - (CONV_WIDTH - 1)
    for lt in range(tiles):
        lanes = slice(lt * LANES, (lt + 1) * LANES)
        taps = [jnp.broadcast_to(w_ref[k:k + 1, lanes], (SUBLANES, LANES)) for k in range(CONV_WIDTH)]
        bias = jnp.broadcast_to(b_ref[:, lanes], (SUBLANES, LANES))

        def rows(r, carry, lt=lt, taps=taps, bias=bias):
            r0 = pl.multiple_of(r * CONV_ROWS, CONV_ROWS)
            acc = [[bias, None] for _ in range(CONV_PITCH)]
            for m in range(CONV_WIDTH + CONV_PITCH - 1):
                win = buf_ref[lt, pl.ds(r0 + first + m, SUBLANES, stride=CONV_PITCH), :]
                for j in range(CONV_PITCH):
                    if 0 <= m - j < CONV_WIDTH:
                        term = taps[m - j] * win
                        part = acc[j][m % 2]
                        acc[j][m % 2] = term if part is None else part + term
            for j in range(CONV_PITCH):
                y_ref[lt, pl.ds(r0 + j, SUBLANES, stride=CONV_PITCH), :] = acc[j][0] + acc[j][1]
            return carry

        lax.fori_loop(0, tm // CONV_ROWS, rows, 0)

    def norm(r, carry):
        r0 = pl.multiple_of(r * LN_ROWS, LN_ROWS)
        y = jnp.concatenate([y_ref[lt, pl.ds(r0, LN_ROWS), :] for lt in range(tiles)], axis=1)
        mu = jnp.mean(y, axis=-1, keepdims=True)
        yc = y - mu
        var = jnp.mean(yc * yc, axis=-1, keepdims=True)
        z = yc * lax.rsqrt(var + EPS) * g_ref[...] + beta_ref[...]
        o_ref[pl.ds(r0, LN_ROWS), :] = (z * jax.nn.sigmoid(z)).astype(BF16)
        return carry

    lax.fori_loop(0, tm // LN_ROWS, norm, 0)


def _conv(a, w, bias, g, beta, *, tm=512):
    b, s, c = a.shape
    tm = min(tm, s)
    assert tm % CONV_ROWS == 0 and tm % CONV_HALO == 0 and tm % LN_ROWS == 0 and c % LANES == 0
    per = tm // CONV_HALO
    tiles = c // LANES

    def const(shape):
        return pl.BlockSpec(shape, lambda bi, i: (0, 0))

    return pl.pallas_call(
        _conv_body,
        grid=(b, s // tm),
        in_specs=[
            pl.BlockSpec((None, tm, c), lambda bi, i: (bi, i, 0)),
            pl.BlockSpec((None, CONV_HALO, c), lambda bi, i: (bi, jnp.maximum(i * per - 1, 0), 0)),
            const(w.shape), const((1, c)), const((1, c)), const((1, c)),
        ],
        out_specs=pl.BlockSpec((None, tm, c), lambda bi, i: (bi, i, 0)),
        out_shape=jax.ShapeDtypeStruct((b, s, c), BF16),
        scratch_shapes=[pltpu.VMEM((tiles, CONV_HALO + tm, LANES), F32),
                        pltpu.VMEM((tiles, tm, LANES), F32)],
        compiler_params=_params(32),
        name="conv_module",
    )(a, a, w, bias, g, beta)


def _outproj_body(x_ref, yc_ref, ym_ref, wt_ref, wb_ref, o_ref):
    o_ref[...] = x_ref[...] + _dot(yc_ref[...], wt_ref[...]) + _dot(ym_ref[...], wb_ref[...])


def _outproj(x, yc, ym, w, layer, *, tm=512):
    b, s, d = x.shape
    tm = min(tm, s)
    cc = yc.shape[-1]
    assert w.shape[1] == 2 * cc

    def tile(c):
        return pl.BlockSpec((None, tm, c), lambda bi, i: (bi, i, 0))

    return pl.pallas_call(
        _outproj_body,
        grid=(b, s // tm),
        in_specs=[tile(d), tile(cc), tile(cc),
                  pl.BlockSpec((None, cc, d), lambda bi, i: (layer, 0, 0)),
                  pl.BlockSpec((None, cc, d), lambda bi, i: (layer, 1, 0))],
        out_specs=tile(d),
        out_shape=jax.ShapeDtypeStruct((b, s, d), F32),
        compiler_params=_params(48),
        name="outproj",
    )(x, yc, ym, w, w)


def _memkv_body(mem_ref, g_ref, wk_ref, wv_ref, gk_ref, k_ref, v_ref):
    m = _rms(mem_ref[...], g_ref[...]).astype(BF16)
    k_ref[...] = _rms(_dot(m, wk_ref[...].astype(BF16)), gk_ref[...]).astype(BF16)
    v_ref[...] = _dot(m, wv_ref[...].astype(BF16)).astype(BF16)


def _memkv(mem, g, wk, wv, layer, gk):
    rows, d = mem.shape
    hd = d // X_HEADS
    col = pl.BlockSpec((None, d, hd), lambda h: (layer, 0, h))
    out = pl.BlockSpec((rows, hd), lambda h: (0, h))
    shape = jax.ShapeDtypeStruct((rows, d), BF16)
    return pl.pallas_call(
        _memkv_body,
        grid=(X_HEADS,),
        in_specs=[pl.BlockSpec((rows, d), lambda h: (0, 0)), pl.BlockSpec((1, d), lambda h: (0, 0)),
                  col, col, pl.BlockSpec((1, hd), lambda h: (0, 0))],
        out_specs=[out, out],
        out_shape=[shape, shape],
        compiler_params=_params(40),
        name="mem_kv",
    )(mem, g, wk, wv, gk)


def _cross_body(x_ref, g_ref, wq_ref, gq_ref, k_ref, v_ref, wo_ref, o_ref, att_ref):
    x = x_ref[...]
    h = _rms(x, g_ref[...]).astype(BF16)
    hd = x.shape[-1] // X_HEADS
    scale = hd ** -0.5 * LOG2E
    cols = [slice(i * hd, (i + 1) * hd) for i in range(X_HEADS)]
    q = [_dot(h, wq_ref[:, c]) for c in cols]
    qn = [(qi * (lax.rsqrt(jnp.mean(qi * qi, axis=-1, keepdims=True) + EPS) * scale)
           * gq_ref[...]).astype(BF16) for qi in q]
    s = [_dot_nt(qi, k_ref[:, c]) for qi, c in zip(qn, cols)]
    p = [jnp.exp2(si - jnp.max(si, axis=-1, keepdims=True)) for si in s]
    for pi, c in zip(p, cols):
        l = jnp.sum(pi, axis=-1, keepdims=True)
        att_ref[:, c] = (_dot(pi.astype(BF16), v_ref[:, c]) / l).astype(BF16)
    o_ref[...] = x + _dot(att_ref[...], wo_ref[...])


def _cross(x, g, wq, gq, k, v, wo, layer, *, tm=512):
    b, s, d = x.shape
    tm = min(tm, s)
    m = k.shape[1]
    hd = d // X_HEADS

    def tile(c):
        return pl.BlockSpec((None, tm, c), lambda bi, i: (bi, i, 0))

    def const(shape):
        return pl.BlockSpec(shape, lambda bi, i: (0, 0))

    weight = pl.BlockSpec((None, d, d), lambda bi, i: (layer, 0, 0), pipeline_mode=pl.Buffered(1))

    mem = pl.BlockSpec((None, m, d), lambda bi, i: (bi, 0, 0))
    return pl.pallas_call(
        _cross_body,
        grid=(b, s // tm),
        in_specs=[tile(d), const((1, d)), weight, const((1, hd)), mem, mem, weight],
        out_specs=tile(d),
        out_shape=jax.ShapeDtypeStruct((b, s, d), F32),
        scratch_shapes=[pltpu.VMEM((tm, d), BF16)],
        compiler_params=_params(56),
        name="cross_attention",
    )(x, g, wq, gq, k, v, wo)


def _pad_cols(w, heads, width, padded):
    k = w.shape[0]
    w = w.reshape(k, heads, width)
    return jnp.pad(w, ((0, 0), (0, 0), (0, padded - width))).reshape(k, heads * padded)


def kernel(x, mem, positions, ffn1_norm, ffn1_w_gate, ffn1_w_up, ffn1_w_down, mix_norm, w_in, conv_w, conv_b, conv_ln_g, conv_ln_b, q_a_norm, w_q_b, kv_a_norm, w_kv_b, q_norm, k_norm, w_out, cross_norm, mem_norm, w_cq, w_ck, w_cv, cq_norm, ck_norm, w_co, ffn2_norm, ffn2_w_gate, ffn2_w_up, ffn2_w_down):
    b, s, d = x.shape
    n_mem = mem.shape[1]
    depth = w_in.shape[0]

    def row(v):
        return v.astype(F32)[None]

    tabs = _rope_tables(positions)
    mem2d = mem.reshape(b * n_mem, d)
    attn_blk = min(ATTN_BLOCK, s)
    ffn1 = [_to_bf16(w) for w in (ffn1_w_gate, ffn1_w_up, ffn1_w_down)]
    ffn2 = [_to_bf16(w) for w in (ffn2_w_gate, ffn2_w_up, ffn2_w_down)]
    w_out16, w_cq16, w_co16 = (_to_bf16(w) for w in (w_out, w_cq, w_co))
    w_in16 = w_in.astype(BF16)

    for l in range(depth):
        x = _ffn(x.reshape(b * s, d), row(ffn1_norm[l]), *ffn1, l).reshape(b, s, d)

        a, cq, ckv, kpe = _inproj(x, row(mix_norm[l]), w_in16, l, row(q_a_norm[l]), row(kv_a_norm[l]))

        wkv = w_kv_b[l].reshape(KV_LORA, MLA_HEADS, QK_NOPE + V_HEAD)
        wk = wkv[:, :, :QK_NOPE].reshape(KV_LORA, MLA_HEADS * QK_NOPE)
        wvt = wkv[:, :, QK_NOPE:].reshape(KV_LORA, MLA_HEADS * V_HEAD).T
        gq = jnp.broadcast_to(jnp.pad(q_norm[l], (0, HEAD_PAD - QK_HEAD))[:, None], (HEAD_PAD, LANES))
        gkp = jnp.pad(k_norm[l][QK_NOPE:], (0, LANES - QK_ROPE))
        q, k, vt = _qkv(cq, ckv, kpe, tabs,
                        _pad_cols(w_q_b[l], MLA_HEADS, QK_HEAD, HEAD_PAD).T.astype(BF16),
                        wk.astype(BF16), wvt.astype(BF16), gq.astype(F32), row(k_norm[l][:QK_NOPE]),
                        row(gkp), tm=attn_blk)
        y_mla = _attention(q, k, vt, blk=attn_blk)

        cw = jnp.pad(conv_w[l], ((0, CONV_HALO - CONV_WIDTH), (0, 0)))
        y_conv = _conv(a, cw, row(conv_b[l]), row(conv_ln_g[l]), row(conv_ln_b[l]))

        x = _outproj(x, y_conv, y_mla, w_out16, l)

        km, vm = _memkv(mem2d, row(mem_norm[l]), w_ck, w_cv, l, row(ck_norm[l]))
        x = _cross(x, row(cross_norm[l]), w_cq16, row(cq_norm[l]),
                   km.reshape(b, n_mem, d), vm.reshape(b, n_mem, d), w_co16, l)

        x = _ffn(x.reshape(b * s, d), row(ffn2_norm[l]), *ffn2, l).reshape(b, s, d)
    return x
```

```python
import functools

import jax
import jax.numpy as jnp
from jax import lax
from jax.experimental import pallas as pl
from jax.experimental.pallas import tpu as pltpu

CHUNK = 64
CONV_WIDTH = 31
MLA_HEADS = 8
QK_NOPE = 128
QK_ROPE = 64
QK_HEAD = QK_NOPE + QK_ROPE
V_HEAD = 128
Q_LORA = 768
KV_LORA = 256
ROPE_THETA = 10000.0
X_HEADS = 4
EPS = 1e-6
NEG_INF = -1e30

LANES = 128
SUBLANES = 8
BF16_ROWS = 16
VMEM_BYTES = 64 * 1024 * 1024
HEAD_PAD = 2 * LANES

F32 = jnp.float32
BF16 = jnp.bfloat16


def _params(vmem_mib):
    assert vmem_mib * 1024 * 1024 <= VMEM_BYTES
    return pltpu.CompilerParams(vmem_limit_bytes=vmem_mib * 1024 * 1024)


def _rms(x, g):
    ms = jnp.mean(x * x, axis=-1, keepdims=True)
    return x * lax.rsqrt(ms + EPS) * g


def _dot(a, b):
    return jnp.dot(a, b, preferred_element_type=F32)


def _dot_nt(a, b):
    return lax.dot_general(a, b, (((1,), (1,)), ((), ())), preferred_element_type=F32)


CAST_BLOCK_BYTES = 8 * 1024 * 1024


def _cast_body(w_ref, o_ref):
    o_ref[...] = w_ref[...].astype(BF16)


def _to_bf16(w):
    layers, k, n = w.shape
    tk = k
    while tk * n * 4 > CAST_BLOCK_BYTES and tk % (2 * BF16_ROWS) == 0:
        tk //= 2
    spec = pl.BlockSpec((None, tk, n), lambda l, i: (l, i, 0))
    return pl.pallas_call(
        _cast_body,
        grid=(layers, k // tk),
        in_specs=[spec],
        out_specs=spec,
        out_shape=jax.ShapeDtypeStruct(w.shape, BF16),
        compiler_params=_params(40),
        name="cast_bf16",
    )(w)


def _ffn_body(x_ref, g_ref, wg_ref, wu_ref, wd_ref, o_ref, h_ref):
    j = pl.program_id(1)

    @pl.when(j == 0)
    def _():
        x = x_ref[...]
        h_ref[...] = _rms(x, g_ref[...]).astype(BF16)
        o_ref[...] = x

    h = h_ref[...]
    gate = _dot(h, wg_ref[...])
    up = _dot(h, wu_ref[...])
    mid = (gate * jax.nn.sigmoid(gate) * (0.5 * up)).astype(BF16)
    o_ref[...] += _dot(mid, wd_ref[...])


def _ffn(x, g, wg, wu, wd, layer, *, tm=1024, tf=512):
    t, d = x.shape
    f = wg.shape[-1]
    tm, tf = min(tm, t), min(tf, f)
    return pl.pallas_call(
        _ffn_body,
        grid=(t // tm, f // tf),
        in_specs=[
            pl.BlockSpec((tm, d), lambda i, j: (i, 0)),
            pl.BlockSpec((1, d), lambda i, j: (0, 0)),
            pl.BlockSpec((None, d, tf), lambda i, j: (layer, 0, j)),
            pl.BlockSpec((None, d, tf), lambda i, j: (layer, 0, j)),
            pl.BlockSpec((None, tf, d), lambda i, j: (layer, j, 0)),
        ],
        out_specs=pl.BlockSpec((tm, d), lambda i, j: (i, 0)),
        out_shape=jax.ShapeDtypeStruct((t, d), F32),
        scratch_shapes=[pltpu.VMEM((tm, d), BF16)],
        compiler_params=_params(60),
        name="ffn",
    )(x, g, wg, wu, wd)


def _rope_tab_body(pos_ref, freq_ref, mc_ref, m1_ref, m2_ref, c_ref, s1_ref, s2_ref, ct_ref, st_ref):
    ang = pos_ref[...].astype(F32) * freq_ref[...]
    cos, sin = jnp.cos(ang), jnp.sin(ang)
    c_ref[...] = cos * mc_ref[...]
    s1_ref[...] = sin * m1_ref[...]
    s2_ref[...] = sin * m2_ref[...]
    ct_ref[...] = cos.T[:QK_ROPE // 2]
    st_ref[...] = sin.T[:QK_ROPE // 2]


def _rope_tables(positions, *, ts=1024):
    b, s = positions.shape
    ts = min(ts, s)
    half = QK_ROPE // 2
    inv_freq = 1.0 / (ROPE_THETA ** (jnp.arange(0, QK_ROPE, 2, dtype=F32) / QK_ROPE))
    zeros = jnp.zeros((LANES - QK_ROPE,), F32)
    ones, zhalf = jnp.ones((half,), F32), jnp.zeros((half,), F32)
    freq = jnp.concatenate([inv_freq, inv_freq, zeros])[None]
    mc = jnp.concatenate([ones, ones, zeros])[None]
    m1 = jnp.concatenate([-ones, zhalf, zeros])[None]
    m2 = jnp.concatenate([zhalf, ones, zeros])[None]
    row = pl.BlockSpec((1, LANES), lambda bi, i: (0, 0))
    tab = pl.BlockSpec((None, ts, LANES), lambda bi, i: (bi, i, 0))
    tab_t = pl.BlockSpec((None, half, ts), lambda bi, i: (bi, 0, i))
    shape = jax.ShapeDtypeStruct((b, s, LANES), F32)
    shape_t = jax.ShapeDtypeStruct((b, half, s), F32)
    return pl.pallas_call(
        _rope_tab_body,
        grid=(b, s // ts),
        in_specs=[pl.BlockSpec((None, ts, 1), lambda bi, i: (bi, i, 0)), row, row, row, row],
        out_specs=[tab, tab, tab, tab_t, tab_t],
        out_shape=[shape, shape, shape, shape_t, shape_t],
        name="rope_tables",
    )(positions[..., None], freq, mc, m1, m2)


def _rope(r, c, s1, s2):
    return r * c + pltpu.roll(r, LANES - QK_ROPE // 2, 1) * s1 + pltpu.roll(r, QK_ROPE // 2, 1) * s2


def _inproj_body(x_ref, g_ref, w_ref, gq_ref, gkv_ref, a_ref, cq_ref, ckv_ref, kpe_ref):
    h = _rms(x_ref[...], g_ref[...]).astype(BF16)
    cc = a_ref.shape[-1]
    o_q, o_kv, o_pe = 2 * cc, 2 * cc + Q_LORA, 2 * cc + Q_LORA + KV_LORA
    a_ref[...] = _dot(h, w_ref[:, :cc]) * jax.nn.sigmoid(_dot(h, w_ref[:, cc:o_q]))
    cq_ref[...] = _rms(_dot(h, w_ref[:, o_q:o_kv]), gq_ref[...]).astype(BF16)
    ckv_ref[...] = _rms(_dot(h, w_ref[:, o_kv:o_pe]), gkv_ref[...]).astype(BF16)
    kpe_ref[:, :QK_ROPE] = _dot(h, w_ref[:, o_pe:])
    kpe_ref[:, QK_ROPE:] = jnp.zeros((kpe_ref.shape[0], LANES - QK_ROPE), F32)


def _inproj(x, g, w, layer, gq, gkv, *, tm=512):
    b, s, d = x.shape
    tm = min(tm, s)
    cols = w.shape[-1]
    cc = (cols - Q_LORA - KV_LORA - QK_ROPE) // 2

    def const(shape):
        return pl.BlockSpec(shape, lambda bi, i: (0, 0))

    def tile(c):
        return pl.BlockSpec((None, tm, c), lambda bi, i: (bi, i, 0))

    return pl.pallas_call(
        _inproj_body,
        grid=(b, s // tm),
        in_specs=[tile(d), const((1, d)),
                  pl.BlockSpec((None, d, cols), lambda bi, i: (layer, 0, 0)),
                  const((1, Q_LORA)), const((1, KV_LORA))],
        out_specs=[tile(cc), tile(Q_LORA), tile(KV_LORA), tile(LANES)],
        out_shape=[jax.ShapeDtypeStruct((b, s, cc), F32),
                   jax.ShapeDtypeStruct((b, s, Q_LORA), BF16),
                   jax.ShapeDtypeStruct((b, s, KV_LORA), BF16),
                   jax.ShapeDtypeStruct((b, s, LANES), F32)],
        compiler_params=_params(56),
        name="inproj",
    )(x, g, w, gq, gkv)


LOG2E = 1.4426950408889634
VT_ROWS = V_HEAD + 16


def _qkv_body(cq_ref, ckv_ref, kpe_ref, c_ref, s1_ref, s2_ref, ct_ref, st_ref, wqt_ref, wk_ref,
              wvt_ref, gq_ref, gkn_ref, gkp_ref, q_ref, k_ref, vt_ref, qt_ref):
    scale = QK_HEAD ** -0.5 * LOG2E
    cq, ckv = cq_ref[...], ckv_ref[...]
    tm = cq.shape[0]
    half = QK_ROPE // 2

    qt_ref[...] = _dot_nt(wqt_ref[...], cq)
    ct, st = ct_ref[...], st_ref[...]
    gq = jnp.tile(gq_ref[...], (1, tm // LANES))
    for hd in range(MLA_HEADS):
        qh = qt_ref[hd * HEAD_PAD:(hd + 1) * HEAD_PAD, :]
        ss = jnp.sum(qh * qh, axis=0, keepdims=True)
        qn = qh * (lax.rsqrt(ss * (1.0 / QK_HEAD) + EPS) * scale) * gq
        x1, x2 = qn[QK_NOPE:QK_NOPE + half], qn[QK_NOPE + half:QK_HEAD]
        q_ref[hd, :QK_NOPE, :] = qn[:QK_NOPE].astype(BF16)
        q_ref[hd, QK_NOPE:QK_NOPE + half, :] = (x1 * ct - x2 * st).astype(BF16)
        q_ref[hd, QK_NOPE + half:QK_HEAD, :] = (x2 * ct + x1 * st).astype(BF16)
        q_ref[hd, QK_HEAD:, :] = jnp.zeros((HEAD_PAD - QK_HEAD, tm), BF16)

    kpe = kpe_ref[...]
    kpe_ss = jnp.sum(kpe * kpe, axis=-1, keepdims=True)
    kpe_rot = _rope(kpe * gkp_ref[...], c_ref[...], s1_ref[...], s2_ref[...])
    for hd in range(MLA_HEADS):
        kn = _dot(ckv, wk_ref[:, hd * QK_NOPE:(hd + 1) * QK_NOPE])
        ss = jnp.sum(kn * kn, axis=-1, keepdims=True) + kpe_ss
        rinv = lax.rsqrt(ss * (1.0 / QK_HEAD) + EPS)
        k_ref[hd, :, :LANES] = (kn * rinv * gkn_ref[...]).astype(BF16)
        k_ref[hd, :, LANES:] = (kpe_rot * rinv).astype(BF16)

        vt = _dot_nt(wvt_ref[hd * V_HEAD:(hd + 1) * V_HEAD, :], ckv)
        vt_ref[hd, :V_HEAD, :] = vt.astype(BF16)
        vt_ref[hd, V_HEAD:, :] = jnp.ones((VT_ROWS - V_HEAD, tm), BF16)


def _qkv(cq, ckv, kpe, tabs, wqt, wk, wvt, gq, gkn, gkp, *, tm):
    b, s, _ = cq.shape
    assert s % tm == 0
    half = QK_ROPE // 2

    def const(shape):
        return pl.BlockSpec(shape, lambda bi, i: (0, 0))

    def tile(c):
        return pl.BlockSpec((None, tm, c), lambda bi, i: (bi, i, 0))

    def heads(c):
        return pl.BlockSpec((None, MLA_HEADS, tm, c), lambda bi, i: (bi, 0, i, 0))

    return pl.pallas_call(
        _qkv_body,
        grid=(b, s // tm),
        in_specs=[tile(Q_LORA), tile(KV_LORA), tile(LANES), tile(LANES), tile(LANES), tile(LANES),
                  pl.BlockSpec((None, half, tm), lambda bi, i: (bi, 0, i)),
                  pl.BlockSpec((None, half, tm), lambda bi, i: (bi, 0, i)),
                  const(wqt.shape), const(wk.shape), const(wvt.shape),
                  const((HEAD_PAD, LANES)), const((1, LANES)), const((1, LANES))],
        out_specs=[pl.BlockSpec((None, MLA_HEADS, None, HEAD_PAD, tm), lambda bi, i: (bi, 0, i, 0, 0)),
                   heads(HEAD_PAD),
                   pl.BlockSpec((None, MLA_HEADS, None, VT_ROWS, tm), lambda bi, i: (bi, 0, i, 0, 0))],
        out_shape=[jax.ShapeDtypeStruct((b, MLA_HEADS, s // tm, HEAD_PAD, tm), BF16),
                   jax.ShapeDtypeStruct((b, MLA_HEADS, s, HEAD_PAD), BF16),
                   jax.ShapeDtypeStruct((b, MLA_HEADS, s // tm, VT_ROWS, tm), BF16)],
        scratch_shapes=[pltpu.VMEM((MLA_HEADS * HEAD_PAD, tm), F32)],
        compiler_params=_params(56),
        name="qkv",
    )(cq, ckv, kpe, *tabs, wqt, wk, wvt, gq, gkn, gkp)


ATTN_QW = 256
ATTN_BLOCK = 1024
ATTN_AHEAD = 3


def _colmax(x):
    while x.shape[0] >= SUBLANES * SUBLANES:
        x = jnp.max(x.reshape(SUBLANES, x.shape[0] // SUBLANES, x.shape[1]), axis=0)
    return jnp.max(x, axis=0, keepdims=True)


def _attn_body(qt_ref, k_ref, vt_ref, o_ref, s_ref, mx_ref, m_ref, acc_ref, *, blk):
    nblk = qt_ref.shape[0]
    chains = blk // ATTN_QW

    def nkeys(c, diag):
        return (c + 1) * ATTN_QW if diag else blk

    def scores(qi, ki, c, diag):
        n = nkeys(c, diag)
        start = pl.multiple_of(ki * blk, blk)
        st = _dot(k_ref[pl.ds(start, n), :], qt_ref[qi, :, c * ATTN_QW:(c + 1) * ATTN_QW])
        s_ref[c, :n, :] = st
        if not diag:
            mx_ref[c] = _colmax(st)

    def softmax_pv(ki, c, diag):
        n = nkeys(c, diag)
        st = s_ref[c, :n, :]
        if diag:
            tail = st[n - ATTN_QW:]
            key = lax.broadcasted_iota(jnp.int32, tail.shape, 0) // CHUNK
            qry = lax.broadcasted_iota(jnp.int32, tail.shape, 1) // CHUNK
            tail = jnp.where(key <= qry, tail, NEG_INF)
            st = tail if n == ATTN_QW else jnp.concatenate([st[:n - ATTN_QW], tail], axis=0)
        m_old = m_ref[c]
        m_new = jnp.maximum(m_old, _colmax(st) if diag else mx_ref[c])
        p = jnp.exp2(st - m_new).astype(BF16)
        acc_ref[c] = jnp.exp2(m_old - m_new) * acc_ref[c] + _dot(vt_ref[ki, :, :n], p)
        m_ref[c] = m_new

    def finalize(qi, c):
        acc = acc_ref[c]
        out = acc[:V_HEAD] / acc[V_HEAD:V_HEAD + 1]
        row = pl.multiple_of(qi * blk + c * ATTN_QW, ATTN_QW)
        o_ref[pl.ds(row, ATTN_QW), :] = out.T.astype(BF16)

    def reset():
        m_ref[...] = jnp.full(m_ref.shape, NEG_INF, F32)
        acc_ref[...] = jnp.zeros(acc_ref.shape, F32)

    def prologue(qi):
        for c in range(ATTN_AHEAD):
            scores(qi, 0, c, False)

    def unmasked(qi, ki):
        for c in range(chains):
            ahead = c + ATTN_AHEAD
            scores(qi, ki + ahead // chains, ahead % chains, False)
            softmax_pv(ki, c, False)

    def query_block(qi, carry):
        def pair(t, inner):
            unmasked(qi, 2 * t)
            unmasked(qi, 2 * t + 1)
            return inner

        lax.fori_loop(0, qi // 2, pair, 0)

        @pl.when(qi % 2 == 1)
        def _():
            unmasked(qi, qi - 1)

        for c in range(chains):
            if c + ATTN_AHEAD < chains:
                scores(qi, qi, c + ATTN_AHEAD, True)
            softmax_pv(qi, c, True)
        prologue(jnp.minimum(qi + 1, nblk - 1))
        for c in range(chains):
            finalize(qi, c)
        reset()
        return carry

    reset()
    prologue(0)
    lax.fori_loop(0, nblk, query_block, 0)


def _attention(qt, k, vt, *, blk):
    b, nh, s, _ = k.shape
    nblk = s // blk
    assert blk % CHUNK == 0 and blk % ATTN_QW == 0
    assert qt.shape[2:] == (nblk, HEAD_PAD, blk) and vt.shape[2:] == (nblk, VT_ROWS, blk)
    chains = blk // ATTN_QW
    return pl.pallas_call(
        functools.partial(_attn_body, blk=blk),
        grid=(b, nh),
        in_specs=[
            pl.BlockSpec((None, None, nblk, HEAD_PAD, blk), lambda bi, h: (bi, h, 0, 0, 0)),
            pl.BlockSpec((None, None, s, HEAD_PAD), lambda bi, h: (bi, h, 0, 0)),
            pl.BlockSpec((None, None, nblk, VT_ROWS, blk), lambda bi, h: (bi, h, 0, 0, 0)),
        ],
        out_specs=pl.BlockSpec((None, s, V_HEAD), lambda bi, h: (bi, 0, h)),
        out_shape=jax.ShapeDtypeStruct((b, s, nh * V_HEAD), BF16),
        scratch_shapes=[pltpu.VMEM((chains, blk, ATTN_QW), F32),
                        pltpu.VMEM((chains, 1, ATTN_QW), F32),
                        pltpu.VMEM((chains, 1, ATTN_QW), F32),
                        pltpu.VMEM((chains, VT_ROWS, ATTN_QW), F32)],
        compiler_params=_params(40),
        name="mla_attention",
    )(qt, k, vt)


CONV_HALO = 32
CONV_PITCH = 4
CONV_ROWS = CONV_PITCH * SUBLANES
LN_ROWS = 256


def _conv_body(a_ref, prev_ref, w_ref, b_ref, g_ref, beta_ref, o_ref, buf_ref, y_ref):
    i = pl.program_id(1)
    tm, c = a_ref.shape
    tiles = c // LANES
    for lt in range(tiles):
        buf_ref[lt, CONV_HALO:, :] = a_ref[:, lt * LANES:(lt + 1) * LANES]

    @pl.when(i == 0)
    def _():
        buf_ref[:, :CONV_HALO, :] = jnp.zeros((tiles, CONV_HALO, LANES), F32)

    @pl.when(i > 0)
    def _():
        for lt in range(tiles):
            buf_ref[lt, :CONV_HALO, :] = prev_ref[:, lt * LANES:(lt + 1) * LANES]

    first = CONV_HALO - (CONV_WIDTH - 1)
    for lt in range(tiles):
        lanes = slice(lt * LANES, (lt + 1) * LANES)
        taps = [jnp.broadcast_to(w_ref[k:k + 1, lanes], (SUBLANES, LANES)) for k in range(CONV_WIDTH)]
        bias = jnp.broadcast_to(b_ref[:, lanes], (SUBLANES, LANES))

        def rows(r, carry, lt=lt, taps=taps, bias=bias):
            r0 = pl.multiple_of(r * CONV_ROWS, CONV_ROWS)
            acc = [[bias, None] for _ in range(CONV_PITCH)]
            for m in range(CONV_WIDTH + CONV_PITCH - 1):
                win = buf_ref[lt, pl.ds(r0 + first + m, SUBLANES, stride=CONV_PITCH), :]
                for j in range(CONV_PITCH):
                    if 0 <= m - j < CONV_WIDTH:
                        term = taps[m - j] * win
                        part = acc[j][m % 2]
                        acc[j][m % 2] = term if part is None else part + term
            for j in range(CONV_PITCH):
                y_ref[lt, pl.ds(r0 + j, SUBLANES, stride=CONV_PITCH), :] = acc[j][0] + acc[j][1]
            return carry

        lax.fori_loop(0, tm // CONV_ROWS, rows, 0)

    def norm(r, carry):
        r0 = pl.multiple_of(r * LN_ROWS, LN_ROWS)
        y = jnp.concatenate([y_ref[lt, pl.ds(r0, LN_ROWS), :] for lt in range(tiles)], axis=1)
        mu = jnp.mean(y, axis=-1, keepdims=True)
        yc = y - mu
        var = jnp.mean(yc * yc, axis=-1, keepdims=True)
        z = yc * lax.rsqrt(var + EPS) * g_ref[...] + beta_ref[...]
        o_ref[pl.ds(r0, LN_ROWS), :] = (z * jax.nn.sigmoid(z)).astype(BF16)
        return carry

    lax.fori_loop(0, tm // LN_ROWS, norm, 0)


def _conv(a, w, bias, g, beta, *, tm=512):
    b, s, c = a.shape
    tm = min(tm, s)
    assert tm % CONV_ROWS == 0 and tm % CONV_HALO == 0 and tm % LN_ROWS == 0 and c % LANES == 0
    per = tm // CONV_HALO
    tiles = c // LANES

    def const(shape):
        return pl.BlockSpec(shape, lambda bi, i: (0, 0))

    return pl.pallas_call(
        _conv_body,
        grid=(b, s // tm),
        in_specs=[
            pl.BlockSpec((None, tm, c), lambda bi, i: (bi, i, 0)),
            pl.BlockSpec((None, CONV_HALO, c), lambda bi, i: (bi, jnp.maximum(i * per - 1, 0), 0)),
            const(w.shape), const((1, c)), const((1, c)), const((1, c)),
        ],
        out_specs=pl.BlockSpec((None, tm, c), lambda bi, i: (bi, i, 0)),
        out_shape=jax.ShapeDtypeStruct((b, s, c), BF16),
        scratch_shapes=[pltpu.VMEM((tiles, CONV_HALO + tm, LANES), F32),
                        pltpu.VMEM((tiles, tm, LANES), F32)],
        compiler_params=_params(32),
        name="conv_module",
    )(a, a, w, bias, g, beta)


def _outproj_body(x_ref, yc_ref, ym_ref, wt_ref, wb_ref, o_ref):
    o_ref[...] = x_ref[...] + _dot(yc_ref[...], wt_ref[...]) + _dot(ym_ref[...], wb_ref[...])


def _outproj(x, yc, ym, w, layer, *, tm=512):
    b, s, d = x.shape
    tm = min(tm, s)
    cc = yc.shape[-1]
    assert w.shape[1] == 2 * cc

    def tile(c):
        return pl.BlockSpec((None, tm, c), lambda bi, i: (bi, i, 0))

    return pl.pallas_call(
        _outproj_body,
        grid=(b, s // tm),
        in_specs=[tile(d), tile(cc), tile(cc),
                  pl.BlockSpec((None, cc, d), lambda bi, i: (layer, 0, 0)),
                  pl.BlockSpec((None, cc, d), lambda bi, i: (layer, 1, 0))],
        out_specs=tile(d),
        out_shape=jax.ShapeDtypeStruct((b, s, d), F32),
        compiler_params=_params(48),
        name="outproj",
    )(x, yc, ym, w, w)


def _memkv_body(mem_ref, g_ref, wk_ref, wv_ref, gk_ref, k_ref, v_ref):
    m = _rms(mem_ref[...], g_ref[...]).astype(BF16)
    k_ref[...] = _rms(_dot(m, wk_ref[...].astype(BF16)), gk_ref[...]).astype(BF16)
    v_ref[...] = _dot(m, wv_ref[...].astype(BF16)).astype(BF16)


def _memkv(mem, g, wk, wv, layer, gk):
    rows, d = mem.shape
    hd = d // X_HEADS
    col = pl.BlockSpec((None, d, hd), lambda h: (layer, 0, h))
    out = pl.BlockSpec((rows, hd), lambda h: (0, h))
    shape = jax.ShapeDtypeStruct((rows, d), BF16)
    return pl.pallas_call(
        _memkv_body,
        grid=(X_HEADS,),
        in_specs=[pl.BlockSpec((rows, d), lambda h: (0, 0)), pl.BlockSpec((1, d), lambda h: (0, 0)),
                  col, col, pl.BlockSpec((1, hd), lambda h: (0, 0))],
        out_specs=[out, out],
        out_shape=[shape, shape],
        compiler_params=_params(40),
        name="mem_kv",
    )(mem, g, wk, wv, gk)


def _cross_body(x_ref, g_ref, wq_ref, gq_ref, k_ref, v_ref, wo_ref, o_ref, att_ref):
    x = x_ref[...]
    h = _rms(x, g_ref[...]).astype(BF16)
    hd = x.shape[-1] // X_HEADS
    scale = hd ** -0.5 * LOG2E
    cols = [slice(i * hd, (i + 1) * hd) for i in range(X_HEADS)]
    q = [_dot(h, wq_ref[:, c]) for c in cols]
    qn = [(qi * (lax.rsqrt(jnp.mean(qi * qi, axis=-1, keepdims=True) + EPS) * scale)
           * gq_ref[...]).astype(BF16) for qi in q]
    s = [_dot_nt(qi, k_ref[:, c]) for qi, c in zip(qn, cols)]
    p = [jnp.exp2(si - jnp.max(si, axis=-1, keepdims=True)) for si in s]
    for pi, c in zip(p, cols):
        l = jnp.sum(pi, axis=-1, keepdims=True)
        att_ref[:, c] = (_dot(pi.astype(BF16), v_ref[:, c]) / l).astype(BF16)
    o_ref[...] = x + _dot(att_ref[...], wo_ref[...])


def _cross(x, g, wq, gq, k, v, wo, layer, *, tm=512):
    b, s, d = x.shape
    tm = min(tm, s)
    m = k.shape[1]
    hd = d // X_HEADS

    def tile(c):
        return pl.BlockSpec((None, tm, c), lambda bi, i: (bi, i, 0))

    def const(shape):
        return pl.BlockSpec(shape, lambda bi, i: (0, 0))

    weight = pl.BlockSpec((None, d, d), lambda bi, i: (layer, 0, 0), pipeline_mode=pl.Buffered(1))

    mem = pl.BlockSpec((None, m, d), lambda bi, i: (bi, 0, 0))
    return pl.pallas_call(
        _cross_body,
        grid=(b, s // tm),
        in_specs=[tile(d), const((1, d)), weight, const((1, hd)), mem, mem, weight],
        out_specs=tile(d),
        out_shape=jax.ShapeDtypeStruct((b, s, d), F32),
        scratch_shapes=[pltpu.VMEM((tm, d), BF16)],
        compiler_params=_params(56),
        name="cross_attention",
    )(x, g, wq, gq, k, v, wo)


def _pad_cols(w, heads, width, padded):
    k = w.shape[0]
    w = w.reshape(k, heads, width)
    return jnp.pad(w, ((0, 0), (0, 0), (0, padded - width))).reshape(k, heads * padded)


def kernel(x, mem, positions, ffn1_norm, ffn1_w_gate, ffn1_w_up, ffn1_w_down, mix_norm, w_in, conv_w, conv_b, conv_ln_g, conv_ln_b, q_a_norm, w_q_b, kv_a_norm, w_kv_b, q_norm, k_norm, w_out, cross_norm, mem_norm, w_cq, w_ck, w_cv, cq_norm, ck_norm, w_co, ffn2_norm, ffn2_w_gate, ffn2_w_up, ffn2_w_down):
    b, s, d = x.shape
    n_mem = mem.shape[1]
    depth = w_in.shape[0]

    def row(v):
        return v.astype(F32)[None]

    tabs = _rope_tables(positions)
    mem2d = mem.reshape(b * n_mem, d)
    attn_blk = min(ATTN_BLOCK, s)
    ffn1 = [_to_bf16(w) for w in (ffn1_w_gate, ffn1_w_up, ffn1_w_down)]
    ffn2 = [_to_bf16(w) for w in (ffn2_w_gate, ffn2_w_up, ffn2_w_down)]
    w_out16, w_cq16, w_co16 = (_to_bf16(w) for w in (w_out, w_cq, w_co))
    w_in16 = w_in.astype(BF16)

    for l in range(depth):
        x = _ffn(x.reshape(b * s, d), row(ffn1_norm[l]), *ffn1, l).reshape(b, s, d)

        a, cq, ckv, kpe = _inproj(x, row(mix_norm[l]), w_in16, l, row(q_a_norm[l]), row(kv_a_norm[l]))

        wkv = w_kv_b[l].reshape(KV_LORA, MLA_HEADS, QK_NOPE + V_HEAD)
        wk = wkv[:, :, :QK_NOPE].reshape(KV_LORA, MLA_HEADS * QK_NOPE)
        wvt = wkv[:, :, QK_NOPE:].reshape(KV_LORA, MLA_HEADS * V_HEAD).T
        gq = jnp.broadcast_to(jnp.pad(q_norm[l], (0, HEAD_PAD - QK_HEAD))[:, None], (HEAD_PAD, LANES))
        gkp = jnp.pad(k_norm[l][QK_NOPE:], (0, LANES - QK_ROPE))
        q, k, vt = _qkv(cq, ckv, kpe, tabs,
                        _pad_cols(w_q_b[l], MLA_HEADS, QK_HEAD, HEAD_PAD).T.astype(BF16),
                        wk.astype(BF16), wvt.astype(BF16), gq.astype(F32), row(k_norm[l][:QK_NOPE]),
                        row(gkp), tm=attn_blk)
        y_mla = _attention(q, k, vt, blk=attn_blk)

        cw = jnp.pad(conv_w[l], ((0, CONV_HALO - CONV_WIDTH), (0, 0)))
        y_conv = _conv(a, cw, row(conv_b[l]), row(conv_ln_g[l]), row(conv_ln_b[l]))

        x = _outproj(x, y_conv, y_mla, w_out16, l)

        km, vm = _memkv(mem2d, row(mem_norm[l]), w_ck, w_cv, l, row(ck_norm[l]))
        x = _cross(x, row(cross_norm[l]), w_cq16, row(cq_norm[l]),
                   km.reshape(b, n_mem, d), vm.reshape(b, n_mem, d), w_co16, l)

        x = _ffn(x.reshape(b * s, d), row(ffn2_norm[l]), *ffn2, l).reshape(b, s, d)
    return x
```

```python
import functools

import jax
import jax.numpy as jnp
from jax import lax
from jax.experimental import pallas as pl
from jax.experimental.pallas import tpu as pltpu

CHUNK = 64
CONV_WIDTH = 31
MLA_HEADS = 8
QK_NOPE = 128
QK_ROPE = 64
QK_HEAD = QK_NOPE + QK_ROPE
V_HEAD = 128
Q_LORA = 768
KV_LORA = 256
ROPE_THETA = 10000.0
X_HEADS = 4
EPS = 1e-6
NEG_INF = -1e30

LANES = 128
SUBLANES = 8
BF16_ROWS = 16
VMEM_BYTES = 64 * 1024 * 1024
HEAD_PAD = 2 * LANES

F32 = jnp.float32
BF16 = jnp.bfloat16


def _params(vmem_mib):
    assert vmem_mib * 1024 * 1024 <= VMEM_BYTES
    return pltpu.CompilerParams(vmem_limit_bytes=vmem_mib * 1024 * 1024)


def _rms(x, g):
    ms = jnp.mean(x * x, axis=-1, keepdims=True)
    return x * lax.rsqrt(ms + EPS) * g


def _dot(a, b):
    return jnp.dot(a, b, preferred_element_type=F32)


def _dot_nt(a, b):
    return lax.dot_general(a, b, (((1,), (1,)), ((), ())), preferred_element_type=F32)


CAST_BLOCK_BYTES = 8 * 1024 * 1024


def _cast_body(w_ref, o_ref):
    o_ref[...] = w_ref[...].astype(BF16)


def _to_bf16(w):
    layers, k, n = w.shape
    tk = k
    while tk * n * 4 > CAST_BLOCK_BYTES and tk % (2 * BF16_ROWS) == 0:
        tk //= 2
    spec = pl.BlockSpec((None, tk, n), lambda l, i: (l, i, 0))
    return pl.pallas_call(
        _cast_body,
        grid=(layers, k // tk),
        in_specs=[spec],
        out_specs=spec,
        out_shape=jax.ShapeDtypeStruct(w.shape, BF16),
        compiler_params=_params(40),
        name="cast_bf16",
    )(w)


def _ffn_body(x_ref, g_ref, wg_ref, wu_ref, wd_ref, o_ref, h_ref):
    j = pl.program_id(1)

    @pl.when(j == 0)
    def _():
        x = x_ref[...]
        h_ref[...] = _rms(x, g_ref[...]).astype(BF16)
        o_ref[...] = x

    h = h_ref[...]
    gate = _dot(h, wg_ref[...])
    up = _dot(h, wu_ref[...])
    mid = (gate * jax.nn.sigmoid(gate) * (0.5 * up)).astype(BF16)
    o_ref[...] += _dot(mid, wd_ref[...])


def _ffn(x, g, wg, wu, wd, layer, *, tm=1024, tf=512):
    t, d = x.shape
    f = wg.shape[-1]
    tm, tf = min(tm, t), min(tf, f)
    return pl.pallas_call(
        _ffn_body,
        grid=(t // tm, f // tf),
        in_specs=[
            pl.BlockSpec((tm, d), lambda i, j: (i, 0)),
            pl.BlockSpec((1, d), lambda i, j: (0, 0)),
            pl.BlockSpec((None, d, tf), lambda i, j: (layer, 0, j)),
            pl.BlockSpec((None, d, tf), lambda i, j: (layer, 0, j)),
            pl.BlockSpec((None, tf, d), lambda i, j: (layer, j, 0)),
        ],
        out_specs=pl.BlockSpec((tm, d), lambda i, j: (i, 0)),
        out_shape=jax.ShapeDtypeStruct((t, d), F32),
        scratch_shapes=[pltpu.VMEM((tm, d), BF16)],
        compiler_params=_params(60),
        name="ffn",
    )(x, g, wg, wu, wd)


def _rope_tab_body(pos_ref, freq_ref, mc_ref, m1_ref, m2_ref, c_ref, s1_ref, s2_ref, ct_ref, st_ref):
    ang = pos_ref[...].astype(F32) * freq_ref[...]
    cos, sin = jnp.cos(ang), jnp.sin(ang)
    c_ref[...] = cos * mc_ref[...]
    s1_ref[...] = sin * m1_ref[...]
    s2_ref[...] = sin * m2_ref[...]
    ct_ref[...] = cos.T[:QK_ROPE // 2]
    st_ref[...] = sin.T[:QK_ROPE // 2]


def _rope_tables(positions, *, ts=1024):
    b, s = positions.shape
    ts = min(ts, s)
    half = QK_ROPE // 2
    inv_freq = 1.0 / (ROPE_THETA ** (jnp.arange(0, QK_ROPE, 2, dtype=F32) / QK_ROPE))
    zeros = jnp.zeros((LANES - QK_ROPE,), F32)
    ones, zhalf = jnp.ones((half,), F32), jnp.zeros((half,), F32)
    freq = jnp.concatenate([inv_freq, inv_freq, zeros])[None]
    mc = jnp.concatenate([ones, ones, zeros])[None]
    m1 = jnp.concatenate([-ones, zhalf, zeros])[None]
    m2 = jnp.concatenate([zhalf, ones, zeros])[None]
    row = pl.BlockSpec((1, LANES), lambda bi, i: (0, 0))
    tab = pl.BlockSpec((None, ts, LANES), lambda bi, i: (bi, i, 0))
    tab_t = pl.BlockSpec((None, half, ts), lambda bi, i: (bi, 0, i))
    shape = jax.ShapeDtypeStruct((b, s, LANES), F32)
    shape_t = jax.ShapeDtypeStruct((b, half, s), F32)
    return pl.pallas_call(
        _rope_tab_body,
        grid=(b, s // ts),
        in_specs=[pl.BlockSpec((None, ts, 1), lambda bi, i: (bi, i, 0)), row, row, row, row],
        out_specs=[tab, tab, tab, tab_t, tab_t],
        out_shape=[shape, shape, shape, shape_t, shape_t],
        name="rope_tables",
    )(positions[..., None], freq, mc, m1, m2)


def _rope(r, c, s1, s2):
    return r * c + pltpu.roll(r, LANES - QK_ROPE // 2, 1) * s1 + pltpu.roll(r, QK_ROPE // 2, 1) * s2


def _inproj_body(x_ref, g_ref, w_ref, gq_ref, gkv_ref, a_ref, cq_ref, ckv_ref, kpe_ref):
    h = _rms(x_ref[...], g_ref[...]).astype(BF16)
    cc = a_ref.shape[-1]
    o_q, o_kv, o_pe = 2 * cc, 2 * cc + Q_LORA, 2 * cc + Q_LORA + KV_LORA
    a_ref[...] = _dot(h, w_ref[:, :cc]) * jax.nn.sigmoid(_dot(h, w_ref[:, cc:o_q]))
    cq_ref[...] = _rms(_dot(h, w_ref[:, o_q:o_kv]), gq_ref[...]).astype(BF16)
    ckv_ref[...] = _rms(_dot(h, w_ref[:, o_kv:o_pe]), gkv_ref[...]).astype(BF16)
    kpe_ref[:, :QK_ROPE] = _dot(h, w_ref[:, o_pe:])
    kpe_ref[:, QK_ROPE:] = jnp.zeros((kpe_ref.shape[0], LANES - QK_ROPE), F32)


def _inproj(x, g, w, layer, gq, gkv, *, tm=1024):
    b, s, d = x.shape
    tm = min(tm, s)
    cols = w.shape[-1]
    cc = (cols - Q_LORA - KV_LORA - QK_ROPE) // 2

    def const(shape):
        return pl.BlockSpec(shape, lambda bi, i: (0, 0))

    def tile(c):
        return pl.BlockSpec((None, tm, c), lambda bi, i: (bi, i, 0))

    return pl.pallas_call(
        _inproj_body,
        grid=(b, s // tm),
        in_specs=[tile(d), const((1, d)),
                  pl.BlockSpec((None, d, cols), lambda bi, i: (layer, 0, 0),
                               pipeline_mode=pl.Buffered(1)),
                  const((1, Q_LORA)), const((1, KV_LORA))],
        out_specs=[tile(cc), tile(Q_LORA), tile(KV_LORA), tile(LANES)],
        out_shape=[jax.ShapeDtypeStruct((b, s, cc), F32),
                   jax.ShapeDtypeStruct((b, s, Q_LORA), BF16),
                   jax.ShapeDtypeStruct((b, s, KV_LORA), BF16),
                   jax.ShapeDtypeStruct((b, s, LANES), F32)],
        compiler_params=_params(56),
        name="inproj",
    )(x, g, w, gq, gkv)


LOG2E = 1.4426950408889634
VT_ROWS = V_HEAD + 16


def _qkv_body(cq_ref, ckv_ref, kpe_ref, c_ref, s1_ref, s2_ref, ct_ref, st_ref, wqt_ref, wk_ref,
              wvt_ref, gq_ref, gkn_ref, gkp_ref, q_ref, k_ref, vt_ref, qt_ref):
    scale = QK_HEAD ** -0.5 * LOG2E
    cq, ckv = cq_ref[...], ckv_ref[...]
    tm = cq.shape[0]
    half = QK_ROPE // 2

    qt_ref[...] = _dot_nt(wqt_ref[...], cq)
    ct, st = ct_ref[...], st_ref[...]
    gq = jnp.tile(gq_ref[...], (1, tm // LANES))
    for hd in range(MLA_HEADS):
        qh = qt_ref[hd * HEAD_PAD:(hd + 1) * HEAD_PAD, :]
        ss = jnp.sum(qh * qh, axis=0, keepdims=True)
        qn = qh * (lax.rsqrt(ss * (1.0 / QK_HEAD) + EPS) * scale) * gq
        x1, x2 = qn[QK_NOPE:QK_NOPE + half], qn[QK_NOPE + half:QK_HEAD]
        q_ref[hd, :QK_NOPE, :] = qn[:QK_NOPE].astype(BF16)
        q_ref[hd, QK_NOPE:QK_NOPE + half, :] = (x1 * ct - x2 * st).astype(BF16)
        q_ref[hd, QK_NOPE + half:QK_HEAD, :] = (x2 * ct + x1 * st).astype(BF16)
        q_ref[hd, QK_HEAD:, :] = jnp.zeros((HEAD_PAD - QK_HEAD, tm), BF16)

    kpe = kpe_ref[...]
    kpe_ss = jnp.sum(kpe * kpe, axis=-1, keepdims=True)
    kpe_rot = _rope(kpe * gkp_ref[...], c_ref[...], s1_ref[...], s2_ref[...])
    for hd in range(MLA_HEADS):
        kn = _dot(ckv, wk_ref[:, hd * QK_NOPE:(hd + 1) * QK_NOPE])
        ss = jnp.sum(kn * kn, axis=-1, keepdims=True) + kpe_ss
        rinv = lax.rsqrt(ss * (1.0 / QK_HEAD) + EPS)
        k_ref[hd, :, :LANES] = (kn * rinv * gkn_ref[...]).astype(BF16)
        k_ref[hd, :, LANES:] = (kpe_rot * rinv).astype(BF16)

        vt = _dot_nt(wvt_ref[hd * V_HEAD:(hd + 1) * V_HEAD, :], ckv)
        vt_ref[hd, :V_HEAD, :] = vt.astype(BF16)
        vt_ref[hd, V_HEAD:, :] = jnp.ones((VT_ROWS - V_HEAD, tm), BF16)


def _qkv(cq, ckv, kpe, tabs, wqt, wk, wvt, gq, gkn, gkp, *, tm):
    b, s, _ = cq.shape
    assert s % tm == 0
    half = QK_ROPE // 2

    def const(shape):
        return pl.BlockSpec(shape, lambda bi, i: (0, 0))

    def tile(c):
        return pl.BlockSpec((None, tm, c), lambda bi, i: (bi, i, 0))

    def heads(c):
        return pl.BlockSpec((None, MLA_HEADS, tm, c), lambda bi, i: (bi, 0, i, 0))

    return pl.pallas_call(
        _qkv_body,
        grid=(b, s // tm),
        in_specs=[tile(Q_LORA), tile(KV_LORA), tile(LANES), tile(LANES), tile(LANES), tile(LANES),
                  pl.BlockSpec((None, half, tm), lambda bi, i: (bi, 0, i)),
                  pl.BlockSpec((None, half, tm), lambda bi, i: (bi, 0, i)),
                  const(wqt.shape), const(wk.shape), const(wvt.shape),
                  const((HEAD_PAD, LANES)), const((1, LANES)), const((1, LANES))],
        out_specs=[pl.BlockSpec((None, MLA_HEADS, None, HEAD_PAD, tm), lambda bi, i: (bi, 0, i, 0, 0)),
                   heads(HEAD_PAD),
                   pl.BlockSpec((None, MLA_HEADS, None, VT_ROWS, tm), lambda bi, i: (bi, 0, i, 0, 0))],
        out_shape=[jax.ShapeDtypeStruct((b, MLA_HEADS, s // tm, HEAD_PAD, tm), BF16),
                   jax.ShapeDtypeStruct((b, MLA_HEADS, s, HEAD_PAD), BF16),
                   jax.ShapeDtypeStruct((b, MLA_HEADS, s // tm, VT_ROWS, tm), BF16)],
        scratch_shapes=[pltpu.VMEM((MLA_HEADS * HEAD_PAD, tm), F32)],
        compiler_params=_params(56),
        name="qkv",
    )(cq, ckv, kpe, *tabs, wqt, wk, wvt, gq, gkn, gkp)


ATTN_QW = 256
ATTN_BLOCK = 1024
ATTN_AHEAD = 3


def _colmax(x):
    while x.shape[0] >= SUBLANES * SUBLANES:
        x = jnp.max(x.reshape(SUBLANES, x.shape[0] // SUBLANES, x.shape[1]), axis=0)
    return jnp.max(x, axis=0, keepdims=True)


def _attn_body(qt_ref, k_ref, vt_ref, o_ref, s_ref, mx_ref, m_ref, acc_ref, *, blk):
    nblk = qt_ref.shape[0]
    chains = blk // ATTN_QW

    def nkeys(c, diag):
        return (c + 1) * ATTN_QW if diag else blk

    def scores(qi, ki, c, diag):
        n = nkeys(c, diag)
        start = pl.multiple_of(ki * blk, blk)
        st = _dot(k_ref[pl.ds(start, n), :], qt_ref[qi, :, c * ATTN_QW:(c + 1) * ATTN_QW])
        s_ref[c, :n, :] = st
        if not diag:
            mx_ref[c] = _colmax(st)

    def softmax_pv(ki, c, diag):
        n = nkeys(c, diag)
        st = s_ref[c, :n, :]
        if diag:
            tail = st[n - ATTN_QW:]
            key = lax.broadcasted_iota(jnp.int32, tail.shape, 0) // CHUNK
            qry = lax.broadcasted_iota(jnp.int32, tail.shape, 1) // CHUNK
            tail = jnp.where(key <= qry, tail, NEG_INF)
            st = tail if n == ATTN_QW else jnp.concatenate([st[:n - ATTN_QW], tail], axis=0)
        m_old = m_ref[c]
        m_new = jnp.maximum(m_old, _colmax(st) if diag else mx_ref[c])
        p = jnp.exp2(st - m_new).astype(BF16)
        acc_ref[c] = jnp.exp2(m_old - m_new) * acc_ref[c] + _dot(vt_ref[ki, :, :n], p)
        m_ref[c] = m_new

    def finalize(qi, c):
        acc = acc_ref[c]
        out = acc[:V_HEAD] / acc[V_HEAD:V_HEAD + 1]
        row = pl.multiple_of(qi * blk + c * ATTN_QW, ATTN_QW)
        o_ref[pl.ds(row, ATTN_QW), :] = out.T.astype(BF16)

    def reset():
        m_ref[...] = jnp.full(m_ref.shape, NEG_INF, F32)
        acc_ref[...] = jnp.zeros(acc_ref.shape, F32)

    def prologue(qi):
        for c in range(ATTN_AHEAD):
            scores(qi, 0, c, False)

    def unmasked(qi, ki):
        for c in range(chains):
            ahead = c + ATTN_AHEAD
            scores(qi, ki + ahead // chains, ahead % chains, False)
            softmax_pv(ki, c, False)

    def query_block(qi, carry):
        def pair(t, inner):
            unmasked(qi, 2 * t)
            unmasked(qi, 2 * t + 1)
            return inner

        lax.fori_loop(0, qi // 2, pair, 0)

        @pl.when(qi % 2 == 1)
        def _():
            unmasked(qi, qi - 1)

        for c in range(chains):
            if c + ATTN_AHEAD < chains:
                scores(qi, qi, c + ATTN_AHEAD, True)
            softmax_pv(qi, c, True)
        prologue(jnp.minimum(qi + 1, nblk - 1))
        for c in range(chains):
            finalize(qi, c)
        reset()
        return carry

    reset()
    prologue(0)
    lax.fori_loop(0, nblk, query_block, 0)


def _attention(qt, k, vt, *, blk):
    b, nh, s, _ = k.shape
    nblk = s // blk
    assert blk % CHUNK == 0 and blk % ATTN_QW == 0
    assert qt.shape[2:] == (nblk, HEAD_PAD, blk) and vt.shape[2:] == (nblk, VT_ROWS, blk)
    chains = blk // ATTN_QW
    return pl.pallas_call(
        functools.partial(_attn_body, blk=blk),
        grid=(b, nh),
        in_specs=[
            pl.BlockSpec((None, None, nblk, HEAD_PAD, blk), lambda bi, h: (bi, h, 0, 0, 0)),
            pl.BlockSpec((None, None, s, HEAD_PAD), lambda bi, h: (bi, h, 0, 0)),
            pl.BlockSpec((None, None, nblk, VT_ROWS, blk), lambda bi, h: (bi, h, 0, 0, 0)),
        ],
        out_specs=pl.BlockSpec((None, s, V_HEAD), lambda bi, h: (bi, 0, h)),
        out_shape=jax.ShapeDtypeStruct((b, s, nh * V_HEAD), BF16),
        scratch_shapes=[pltpu.VMEM((chains, blk, ATTN_QW), F32),
                        pltpu.VMEM((chains, 1, ATTN_QW), F32),
                        pltpu.VMEM((chains, 1, ATTN_QW), F32),
                        pltpu.VMEM((chains, VT_ROWS, ATTN_QW), F32)],
        compiler_params=_params(40),
        name="mla_attention",
    )(qt, k, vt)


CONV_HALO = 32
CONV_PITCH = 4
CONV_ROWS = CONV_PITCH * SUBLANES
LN_ROWS = 256


def _conv_body(a_ref, prev_ref, w_ref, b_ref, g_ref, beta_ref, o_ref, buf_ref, y_ref):
    i = pl.program_id(1)
    tm, c = a_ref.shape
    tiles = c // LANES
    for lt in range(tiles):
        buf_ref[lt, CONV_HALO:, :] = a_ref[:, lt * LANES:(lt + 1) * LANES]

    @pl.when(i == 0)
    def _():
        buf_ref[:, :CONV_HALO, :] = jnp.zeros((tiles, CONV_HALO, LANES), F32)

    @pl.when(i > 0)
    def _():
        for lt in range(tiles):
            buf_ref[lt, :CONV_HALO, :] = prev_ref[:, lt * LANES:(lt + 1) * LANES]

    first = CONV_HALO - (CONV_WIDTH - 1)
    for lt in range(tiles):
        lanes = slice(lt * LANES, (lt + 1) * LANES)
        taps = [jnp.broadcast_to(w_ref[k:k + 1, lanes], (SUBLANES, LANES)) for k in range(CONV_WIDTH)]
        bias = jnp.broadcast_to(b_ref[:, lanes], (SUBLANES, LANES))

        def rows(r, carry, lt=lt, taps=taps, bias=bias):
            r0 = pl.multiple_of(r * CONV_ROWS, CONV_ROWS)
            acc = [[bias, None] for _ in range(CONV_PITCH)]
            for m in range(CONV_WIDTH + CONV_PITCH - 1):
                win = buf_ref[lt, pl.ds(r0 + first + m, SUBLANES, stride=CONV_PITCH), :]
                for j in range(CONV_PITCH):
                    if 0 <= m - j < CONV_WIDTH:
                        term = taps[m - j] * win
                        part = acc[j][m % 2]
                        acc[j][m % 2] = term if part is None else part + term
            for j in range(CONV_PITCH):
                y_ref[lt, pl.ds(r0 + j, SUBLANES, stride=CONV_PITCH), :] = acc[j][0] + acc[j][1]
            return carry

        lax.fori_loop(0, tm // CONV_ROWS, rows, 0)

    def norm(r, carry):
        r0 = pl.multiple_of(r * LN_ROWS, LN_ROWS)
        y = jnp.concatenate([y_ref[lt, pl.ds(r0, LN_ROWS), :] for lt in range(tiles)], axis=1)
        mu = jnp.mean(y, axis=-1, keepdims=True)
        yc = y - mu
        var = jnp.mean(yc * yc, axis=-1, keepdims=True)
        z = yc * lax.rsqrt(var + EPS) * g_ref[...] + beta_ref[...]
        o_ref[pl.ds(r0, LN_ROWS), :] = (z * jax.nn.sigmoid(z)).astype(BF16)
        return carry

    lax.fori_loop(0, tm // LN_ROWS, norm, 0)


def _conv(a, w, bias, g, beta, *, tm=512):
    b, s, c = a.shape
    tm = min(tm, s)
    assert tm % CONV_ROWS == 0 and tm % CONV_HALO == 0 and tm % LN_ROWS == 0 and c % LANES == 0
    per = tm // CONV_HALO
    tiles = c // LANES

    def const(shape):
        return pl.BlockSpec(shape, lambda bi, i: (0, 0))

    return pl.pallas_call(
        _conv_body,
        grid=(b, s // tm),
        in_specs=[
            pl.BlockSpec((None, tm, c), lambda bi, i: (bi, i, 0)),
            pl.BlockSpec((None, CONV_HALO, c), lambda bi, i: (bi, jnp.maximum(i * per - 1, 0), 0)),
            const(w.shape), const((1, c)), const((1, c)), const((1, c)),
        ],
        out_specs=pl.BlockSpec((None, tm, c), lambda bi, i: (bi, i, 0)),
        out_shape=jax.ShapeDtypeStruct((b, s, c), BF16),
        scratch_shapes=[pltpu.VMEM((tiles, CONV_HALO + tm, LANES), F32),
                        pltpu.VMEM((tiles, tm, LANES), F32)],
        compiler_params=_params(32),
        name="conv_module",
    )(a, a, w, bias, g, beta)


def _outproj_body(x_ref, yc_ref, ym_ref, wt_ref, wb_ref, o_ref):
    o_ref[...] = x_ref[...] + _dot(yc_ref[...], wt_ref[...]) + _dot(ym_ref[...], wb_ref[...])


def _outproj(x, yc, ym, w, layer, *, tm=512):
    b, s, d = x.shape
    tm = min(tm, s)
    cc = yc.shape[-1]
    assert w.shape[1] == 2 * cc

    def tile(c):
        return pl.BlockSpec((None, tm, c), lambda bi, i: (bi, i, 0))

    return pl.pallas_call(
        _outproj_body,
        grid=(b, s // tm),
        in_specs=[tile(d), tile(cc), tile(cc),
                  pl.BlockSpec((None, cc, d), lambda bi, i: (layer, 0, 0)),
                  pl.BlockSpec((None, cc, d), lambda bi, i: (layer, 1, 0))],
        out_specs=tile(d),
        out_shape=jax.ShapeDtypeStruct((b, s, d), F32),
        compiler_params=_params(48),
        name="outproj",
    )(x, yc, ym, w, w)


def _memkv_body(mem_ref, g_ref, wk_ref, wv_ref, gk_ref, k_ref, v_ref):
    m = _rms(mem_ref[...], g_ref[...]).astype(BF16)
    k_ref[...] = _rms(_dot(m, wk_ref[...].astype(BF16)), gk_ref[...]).astype(BF16)
    v_ref[...] = _dot(m, wv_ref[...].astype(BF16)).astype(BF16)


def _memkv(mem, g, wk, wv, layer, gk):
    rows, d = mem.shape
    hd = d // X_HEADS
    col = pl.BlockSpec((None, d, hd), lambda h: (layer, 0, h))
    out = pl.BlockSpec((rows, hd), lambda h: (0, h))
    shape = jax.ShapeDtypeStruct((rows, d), BF16)
    return pl.pallas_call(
        _memkv_body,
        grid=(X_HEADS,),
        in_specs=[pl.BlockSpec((rows, d), lambda h: (0, 0)), pl.BlockSpec((1, d), lambda h: (0, 0)),
                  col, col, pl.BlockSpec((1, hd), lambda h: (0, 0))],
        out_specs=[out, out],
        out_shape=[shape, shape],
        compiler_params=_params(40),
        name="mem_kv",
    )(mem, g, wk, wv, gk)


def _cross_body(x_ref, g_ref, wq_ref, gq_ref, k_ref, v_ref, wo_ref, o_ref, att_ref):
    x = x_ref[...]
    h = _rms(x, g_ref[...]).astype(BF16)
    hd = x.shape[-1] // X_HEADS
    scale = hd ** -0.5 * LOG2E
    cols = [slice(i * hd, (i + 1) * hd) for i in range(X_HEADS)]
    q = [_dot(h, wq_ref[:, c]) for c in cols]
    qn = [(qi * (lax.rsqrt(jnp.mean(qi * qi, axis=-1, keepdims=True) + EPS) * scale)
           * gq_ref[...]).astype(BF16) for qi in q]
    s = [_dot_nt(qi, k_ref[:, c]) for qi, c in zip(qn, cols)]
    p = [jnp.exp2(si - jnp.max(si, axis=-1, keepdims=True)) for si in s]
    for pi, c in zip(p, cols):
        l = jnp.sum(pi, axis=-1, keepdims=True)
        att_ref[:, c] = (_dot(pi.astype(BF16), v_ref[:, c]) / l).astype(BF16)
    o_ref[...] = x + _dot(att_ref[...], wo_ref[...])


def _cross(x, g, wq, gq, k, v, wo, layer, *, tm=512):
    b, s, d = x.shape
    tm = min(tm, s)
    m = k.shape[1]
    hd = d // X_HEADS

    def tile(c):
        return pl.BlockSpec((None, tm, c), lambda bi, i: (bi, i, 0))

    def const(shape):
        return pl.BlockSpec(shape, lambda bi, i: (0, 0))

    weight = pl.BlockSpec((None, d, d), lambda bi, i: (layer, 0, 0), pipeline_mode=pl.Buffered(1))

    mem = pl.BlockSpec((None, m, d), lambda bi, i: (bi, 0, 0))
    return pl.pallas_call(
        _cross_body,
        grid=(b, s // tm),
        in_specs=[tile(d), const((1, d)), weight, const((1, hd)), mem, mem, weight],
        out_specs=tile(d),
        out_shape=jax.ShapeDtypeStruct((b, s, d), F32),
        scratch_shapes=[pltpu.VMEM((tm, d), BF16)],
        compiler_params=_params(56),
        name="cross_attention",
    )(x, g, wq, gq, k, v, wo)


def _pad_cols(w, heads, width, padded):
    k = w.shape[0]
    w = w.reshape(k, heads, width)
    return jnp.pad(w, ((0, 0), (0, 0), (0, padded - width))).reshape(k, heads * padded)


def kernel(x, mem, positions, ffn1_norm, ffn1_w_gate, ffn1_w_up, ffn1_w_down, mix_norm, w_in, conv_w, conv_b, conv_ln_g, conv_ln_b, q_a_norm, w_q_b, kv_a_norm, w_kv_b, q_norm, k_norm, w_out, cross_norm, mem_norm, w_cq, w_ck, w_cv, cq_norm, ck_norm, w_co, ffn2_norm, ffn2_w_gate, ffn2_w_up, ffn2_w_down):
    b, s, d = x.shape
    n_mem = mem.shape[1]
    depth = w_in.shape[0]

    def row(v):
        return v.astype(F32)[None]

    tabs = _rope_tables(positions)
    mem2d = mem.reshape(b * n_mem, d)
    attn_blk = min(ATTN_BLOCK, s)
    ffn1 = [_to_bf16(w) for w in (ffn1_w_gate, ffn1_w_up, ffn1_w_down)]
    ffn2 = [_to_bf16(w) for w in (ffn2_w_gate, ffn2_w_up, ffn2_w_down)]
    w_out16, w_cq16, w_co16 = (_to_bf16(w) for w in (w_out, w_cq, w_co))
    w_in16 = w_in.astype(BF16)

    for l in range(depth):
        x = _ffn(x.reshape(b * s, d), row(ffn1_norm[l]), *ffn1, l).reshape(b, s, d)

        a, cq, ckv, kpe = _inproj(x, row(mix_norm[l]), w_in16, l, row(q_a_norm[l]), row(kv_a_norm[l]))

        wkv = w_kv_b[l].reshape(KV_LORA, MLA_HEADS, QK_NOPE + V_HEAD)
        wk = wkv[:, :, :QK_NOPE].reshape(KV_LORA, MLA_HEADS * QK_NOPE)
        wvt = wkv[:, :, QK_NOPE:].reshape(KV_LORA, MLA_HEADS * V_HEAD).T
        gq = jnp.broadcast_to(jnp.pad(q_norm[l], (0, HEAD_PAD - QK_HEAD))[:, None], (HEAD_PAD, LANES))
        gkp = jnp.pad(k_norm[l][QK_NOPE:], (0, LANES - QK_ROPE))
        q, k, vt = _qkv(cq, ckv, kpe, tabs,
                        _pad_cols(w_q_b[l], MLA_HEADS, QK_HEAD, HEAD_PAD).T.astype(BF16),
                        wk.astype(BF16), wvt.astype(BF16), gq.astype(F32), row(k_norm[l][:QK_NOPE]),
                        row(gkp), tm=attn_blk)
        y_mla = _attention(q, k, vt, blk=attn_blk)

        cw = jnp.pad(conv_w[l], ((0, CONV_HALO - CONV_WIDTH), (0, 0)))
        y_conv = _conv(a, cw, row(conv_b[l]), row(conv_ln_g[l]), row(conv_ln_b[l]))

        x = _outproj(x, y_conv, y_mla, w_out16, l)

        km, vm = _memkv(mem2d, row(mem_norm[l]), w_ck, w_cv, l, row(ck_norm[l]))
        x = _cross(x, row(cross_norm[l]), w_cq16, row(cq_norm[l]),
                   km.reshape(b, n_mem, d), vm.reshape(b, n_mem, d), w_co16, l)

        x = _ffn(x.reshape(b * s, d), row(ffn2_norm[l]), *ffn2, l).reshape(b, s, d)
    return x
```

```python
import functools

import jax
import jax.numpy as jnp
from jax import lax
from jax.experimental import pallas as pl
from jax.experimental.pallas import tpu as pltpu

CHUNK = 64
CONV_WIDTH = 31
MLA_HEADS = 8
QK_NOPE = 128
QK_ROPE = 64
QK_HEAD = QK_NOPE + QK_ROPE
V_HEAD = 128
Q_LORA = 768
KV_LORA = 256
ROPE_THETA = 10000.0
X_HEADS = 4
EPS = 1e-6
NEG_INF = -1e30

LANES = 128
SUBLANES = 8
BF16_ROWS = 16
VMEM_BYTES = 64 * 1024 * 1024
HEAD_PAD = 2 * LANES

F32 = jnp.float32
BF16 = jnp.bfloat16


def _params(vmem_mib):
    assert vmem_mib * 1024 * 1024 <= VMEM_BYTES
    return pltpu.CompilerParams(vmem_limit_bytes=vmem_mib * 1024 * 1024)


def _rms(x, g):
    ms = jnp.mean(x * x, axis=-1, keepdims=True)
    return x * lax.rsqrt(ms + EPS) * g


def _dot(a, b):
    return jnp.dot(a, b, preferred_element_type=F32)


def _dot_nt(a, b):
    return lax.dot_general(a, b, (((1,), (1,)), ((), ())), preferred_element_type=F32)


CAST_BLOCK_BYTES = 8 * 1024 * 1024


def _cast_body(w_ref, o_ref):
    o_ref[...] = w_ref[...].astype(BF16)


def _to_bf16(w):
    layers, k, n = w.shape
    tk = k
    while tk * n * 4 > CAST_BLOCK_BYTES and tk % (2 * BF16_ROWS) == 0:
        tk //= 2
    spec = pl.BlockSpec((None, tk, n), lambda l, i: (l, i, 0))
    return pl.pallas_call(
        _cast_body,
        grid=(layers, k // tk),
        in_specs=[spec],
        out_specs=spec,
        out_shape=jax.ShapeDtypeStruct(w.shape, BF16),
        compiler_params=_params(40),
        name="cast_bf16",
    )(w)


def _ffn_body(x_ref, g_ref, wg_ref, wu_ref, wd_ref, o_ref, h_ref):
    j = pl.program_id(1)

    @pl.when(j == 0)
    def _():
        x = x_ref[...]
        h_ref[...] = _rms(x, g_ref[...]).astype(BF16)
        o_ref[...] = x

    h = h_ref[...]
    gate = _dot(h, wg_ref[...].astype(BF16))
    up = _dot(h, wu_ref[...].astype(BF16))
    mid = (gate * jax.nn.sigmoid(gate) * (0.5 * up)).astype(BF16)
    o_ref[...] += _dot(mid, wd_ref[...].astype(BF16))


def _ffn(x, g, wg, wu, wd, layer, *, tm=1024, tf=256):
    t, d = x.shape
    f = wg.shape[-1]
    tm, tf = min(tm, t), min(tf, f)
    return pl.pallas_call(
        _ffn_body,
        grid=(t // tm, f // tf),
        in_specs=[
            pl.BlockSpec((tm, d), lambda i, j: (i, 0)),
            pl.BlockSpec((1, d), lambda i, j: (0, 0)),
            pl.BlockSpec((None, d, tf), lambda i, j: (layer, 0, j)),
            pl.BlockSpec((None, d, tf), lambda i, j: (layer, 0, j)),
            pl.BlockSpec((None, tf, d), lambda i, j: (layer, j, 0)),
        ],
        out_specs=pl.BlockSpec((tm, d), lambda i, j: (i, 0)),
        out_shape=jax.ShapeDtypeStruct((t, d), F32),
        scratch_shapes=[pltpu.VMEM((tm, d), BF16)],
        compiler_params=_params(60),
        name="ffn",
    )(x, g, wg, wu, wd)


def _rope_tab_body(pos_ref, freq_ref, mc_ref, m1_ref, m2_ref, c_ref, s1_ref, s2_ref, ct_ref, st_ref):
    ang = pos_ref[...].astype(F32) * freq_ref[...]
    cos, sin = jnp.cos(ang), jnp.sin(ang)
    c_ref[...] = cos * mc_ref[...]
    s1_ref[...] = sin * m1_ref[...]
    s2_ref[...] = sin * m2_ref[...]
    ct_ref[...] = cos.T[:QK_ROPE // 2]
    st_ref[...] = sin.T[:QK_ROPE // 2]


def _rope_tables(positions, *, ts=1024):
    b, s = positions.shape
    ts = min(ts, s)
    half = QK_ROPE // 2
    inv_freq = 1.0 / (ROPE_THETA ** (jnp.arange(0, QK_ROPE, 2, dtype=F32) / QK_ROPE))
    zeros = jnp.zeros((LANES - QK_ROPE,), F32)
    ones, zhalf = jnp.ones((half,), F32), jnp.zeros((half,), F32)
    freq = jnp.concatenate([inv_freq, inv_freq, zeros])[None]
    mc = jnp.concatenate([ones, ones, zeros])[None]
    m1 = jnp.concatenate([-ones, zhalf, zeros])[None]
    m2 = jnp.concatenate([zhalf, ones, zeros])[None]
    row = pl.BlockSpec((1, LANES), lambda bi, i: (0, 0))
    tab = pl.BlockSpec((None, ts, LANES), lambda bi, i: (bi, i, 0))
    tab_t = pl.BlockSpec((None, half, ts), lambda bi, i: (bi, 0, i))
    shape = jax.ShapeDtypeStruct((b, s, LANES), F32)
    shape_t = jax.ShapeDtypeStruct((b, half, s), F32)
    return pl.pallas_call(
        _rope_tab_body,
        grid=(b, s // ts),
        in_specs=[pl.BlockSpec((None, ts, 1), lambda bi, i: (bi, i, 0)), row, row, row, row],
        out_specs=[tab, tab, tab, tab_t, tab_t],
        out_shape=[shape, shape, shape, shape_t, shape_t],
        name="rope_tables",
    )(positions[..., None], freq, mc, m1, m2)


def _rope(r, c, s1, s2):
    return r * c + pltpu.roll(r, LANES - QK_ROPE // 2, 1) * s1 + pltpu.roll(r, QK_ROPE // 2, 1) * s2


def _inproj_body(x_ref, g_ref, w_ref, gq_ref, gkv_ref, a_ref, cq_ref, ckv_ref, kpe_ref):
    h = _rms(x_ref[...], g_ref[...]).astype(BF16)
    cc = a_ref.shape[-1]
    o_q, o_kv, o_pe = 2 * cc, 2 * cc + Q_LORA, 2 * cc + Q_LORA + KV_LORA
    a_ref[...] = _dot(h, w_ref[:, :cc]) * jax.nn.sigmoid(_dot(h, w_ref[:, cc:o_q]))
    cq_ref[...] = _rms(_dot(h, w_ref[:, o_q:o_kv]), gq_ref[...]).astype(BF16)
    ckv_ref[...] = _rms(_dot(h, w_ref[:, o_kv:o_pe]), gkv_ref[...]).astype(BF16)
    kpe_ref[:, :QK_ROPE] = _dot(h, w_ref[:, o_pe:])
    kpe_ref[:, QK_ROPE:] = jnp.zeros((kpe_ref.shape[0], LANES - QK_ROPE), F32)


def _inproj(x, g, w, layer, gq, gkv, *, tm=1024):
    b, s, d = x.shape
    tm = min(tm, s)
    cols = w.shape[-1]
    cc = (cols - Q_LORA - KV_LORA - QK_ROPE) // 2

    def const(shape):
        return pl.BlockSpec(shape, lambda bi, i: (0, 0))

    def tile(c):
        return pl.BlockSpec((None, tm, c), lambda bi, i: (bi, i, 0))

    return pl.pallas_call(
        _inproj_body,
        grid=(b, s // tm),
        in_specs=[tile(d), const((1, d)),
                  pl.BlockSpec((None, d, cols), lambda bi, i: (layer, 0, 0),
                               pipeline_mode=pl.Buffered(1)),
                  const((1, Q_LORA)), const((1, KV_LORA))],
        out_specs=[tile(cc), tile(Q_LORA), tile(KV_LORA), tile(LANES)],
        out_shape=[jax.ShapeDtypeStruct((b, s, cc), F32),
                   jax.ShapeDtypeStruct((b, s, Q_LORA), BF16),
                   jax.ShapeDtypeStruct((b, s, KV_LORA), BF16),
                   jax.ShapeDtypeStruct((b, s, LANES), F32)],
        compiler_params=_params(56),
        name="inproj",
    )(x, g, w, gq, gkv)


LOG2E = 1.4426950408889634
VT_ROWS = V_HEAD + 16


def _qkv_body(cq_ref, ckv_ref, kpe_ref, c_ref, s1_ref, s2_ref, ct_ref, st_ref, wqt_ref, wk_ref,
              wvt_ref, gq_ref, gkn_ref, gkp_ref, q_ref, k_ref, vt_ref, qt_ref):
    scale = QK_HEAD ** -0.5 * LOG2E
    cq, ckv = cq_ref[...], ckv_ref[...]
    tm = cq.shape[0]
    half = QK_ROPE // 2

    qt_ref[...] = _dot_nt(wqt_ref[...], cq)
    ct, st = ct_ref[...], st_ref[...]
    gq = jnp.tile(gq_ref[...], (1, tm // LANES))
    for hd in range(MLA_HEADS):
        qh = qt_ref[hd * HEAD_PAD:(hd + 1) * HEAD_PAD, :]
        ss = jnp.sum(qh * qh, axis=0, keepdims=True)
        qn = qh * (lax.rsqrt(ss * (1.0 / QK_HEAD) + EPS) * scale) * gq
        x1, x2 = qn[QK_NOPE:QK_NOPE + half], qn[QK_NOPE + half:QK_HEAD]
        q_ref[hd, :QK_NOPE, :] = qn[:QK_NOPE].astype(BF16)
        q_ref[hd, QK_NOPE:QK_NOPE + half, :] = (x1 * ct - x2 * st).astype(BF16)
        q_ref[hd, QK_NOPE + half:QK_HEAD, :] = (x2 * ct + x1 * st).astype(BF16)
        q_ref[hd, QK_HEAD:, :] = jnp.zeros((HEAD_PAD - QK_HEAD, tm), BF16)

    kpe = kpe_ref[...]
    kpe_ss = jnp.sum(kpe * kpe, axis=-1, keepdims=True)
    kpe_rot = _rope(kpe * gkp_ref[...], c_ref[...], s1_ref[...], s2_ref[...])
    for hd in range(MLA_HEADS):
        kn = _dot(ckv, wk_ref[:, hd * QK_NOPE:(hd + 1) * QK_NOPE])
        ss = jnp.sum(kn * kn, axis=-1, keepdims=True) + kpe_ss
        rinv = lax.rsqrt(ss * (1.0 / QK_HEAD) + EPS)
        k_ref[hd, :, :LANES] = (kn * rinv * gkn_ref[...]).astype(BF16)
        k_ref[hd, :, LANES:] = (kpe_rot * rinv).astype(BF16)

        vt = _dot_nt(wvt_ref[hd * V_HEAD:(hd + 1) * V_HEAD, :], ckv)
        vt_ref[hd, :V_HEAD, :] = vt.astype(BF16)
        vt_ref[hd, V_HEAD:, :] = jnp.ones((VT_ROWS - V_HEAD, tm), BF16)


def _qkv(cq, ckv, kpe, tabs, wqt, wk, wvt, gq, gkn, gkp, *, tm):
    b, s, _ = cq.shape
    assert s % tm == 0
    half = QK_ROPE // 2

    def const(shape):
        return pl.BlockSpec(shape, lambda bi, i: (0, 0))

    def tile(c):
        return pl.BlockSpec((None, tm, c), lambda bi, i: (bi, i, 0))

    def heads(c):
        return pl.BlockSpec((None, MLA_HEADS, tm, c), lambda bi, i: (bi, 0, i, 0))

    return pl.pallas_call(
        _qkv_body,
        grid=(b, s // tm),
        in_specs=[tile(Q_LORA), tile(KV_LORA), tile(LANES), tile(LANES), tile(LANES), tile(LANES),
                  pl.BlockSpec((None, half, tm), lambda bi, i: (bi, 0, i)),
                  pl.BlockSpec((None, half, tm), lambda bi, i: (bi, 0, i)),
                  const(wqt.shape), const(wk.shape), const(wvt.shape),
                  const((HEAD_PAD, LANES)), const((1, LANES)), const((1, LANES))],
        out_specs=[pl.BlockSpec((None, MLA_HEADS, None, HEAD_PAD, tm), lambda bi, i: (bi, 0, i, 0, 0)),
                   heads(HEAD_PAD),
                   pl.BlockSpec((None, MLA_HEADS, None, VT_ROWS, tm), lambda bi, i: (bi, 0, i, 0, 0))],
        out_shape=[jax.ShapeDtypeStruct((b, MLA_HEADS, s // tm, HEAD_PAD, tm), BF16),
                   jax.ShapeDtypeStruct((b, MLA_HEADS, s, HEAD_PAD), BF16),
                   jax.ShapeDtypeStruct((b, MLA_HEADS, s // tm, VT_ROWS, tm), BF16)],
        scratch_shapes=[pltpu.VMEM((MLA_HEADS * HEAD_PAD, tm), F32)],
        compiler_params=_params(56),
        name="qkv",
    )(cq, ckv, kpe, *tabs, wqt, wk, wvt, gq, gkn, gkp)


ATTN_QW = 256
ATTN_BLOCK = 1024
ATTN_AHEAD = 3


def _colmax(x):
    while x.shape[0] >= SUBLANES * SUBLANES:
        x = jnp.max(x.reshape(SUBLANES, x.shape[0] // SUBLANES, x.shape[1]), axis=0)
    return jnp.max(x, axis=0, keepdims=True)


def _attn_body(qt_ref, k_ref, vt_ref, o_ref, s_ref, mx_ref, m_ref, acc_ref, *, blk):
    nblk = qt_ref.shape[0]
    chains = blk // ATTN_QW

    def nkeys(c, diag):
        return (c + 1) * ATTN_QW if diag else blk

    def scores(qi, ki, c, diag):
        n = nkeys(c, diag)
        start = pl.multiple_of(ki * blk, blk)
        st = _dot(k_ref[pl.ds(start, n), :], qt_ref[qi, :, c * ATTN_QW:(c + 1) * ATTN_QW])
        s_ref[c, :n, :] = st
        if not diag:
            mx_ref[c] = _colmax(st)

    def softmax_pv(ki, c, diag):
        n = nkeys(c, diag)
        st = s_ref[c, :n, :]
        if diag:
            tail = st[n - ATTN_QW:]
            key = lax.broadcasted_iota(jnp.int32, tail.shape, 0) // CHUNK
            qry = lax.broadcasted_iota(jnp.int32, tail.shape, 1) // CHUNK
            tail = jnp.where(key <= qry, tail, NEG_INF)
            st = tail if n == ATTN_QW else jnp.concatenate([st[:n - ATTN_QW], tail], axis=0)
        m_old = m_ref[c]
        m_new = jnp.maximum(m_old, _colmax(st) if diag else mx_ref[c])
        p = jnp.exp2(st - m_new).astype(BF16)
        acc_ref[c] = jnp.exp2(m_old - m_new) * acc_ref[c] + _dot(vt_ref[ki, :, :n], p)
        m_ref[c] = m_new

    def finalize(qi, c):
        acc = acc_ref[c]
        out = acc[:V_HEAD] / acc[V_HEAD:V_HEAD + 1]
        row = pl.multiple_of(qi * blk + c * ATTN_QW, ATTN_QW)
        o_ref[pl.ds(row, ATTN_QW), :] = out.T.astype(BF16)

    def reset():
        m_ref[...] = jnp.full(m_ref.shape, NEG_INF, F32)
        acc_ref[...] = jnp.zeros(acc_ref.shape, F32)

    def prologue(qi):
        for c in range(ATTN_AHEAD):
            scores(qi, 0, c, False)

    def unmasked(qi, ki):
        for c in range(chains):
            ahead = c + ATTN_AHEAD
            scores(qi, ki + ahead // chains, ahead % chains, False)
            softmax_pv(ki, c, False)

    def query_block(qi, carry):
        def pair(t, inner):
            unmasked(qi, 2 * t)
            unmasked(qi, 2 * t + 1)
            return inner

        lax.fori_loop(0, qi // 2, pair, 0)

        @pl.when(qi % 2 == 1)
        def _():
            unmasked(qi, qi - 1)

        for c in range(chains):
            if c + ATTN_AHEAD < chains:
                scores(qi, qi, c + ATTN_AHEAD, True)
            softmax_pv(qi, c, True)
        prologue(jnp.minimum(qi + 1, nblk - 1))
        for c in range(chains):
            finalize(qi, c)
        reset()
        return carry

    reset()
    prologue(0)
    lax.fori_loop(0, nblk, query_block, 0)


def _attention(qt, k, vt, *, blk):
    b, nh, s, _ = k.shape
    nblk = s // blk
    assert blk % CHUNK == 0 and blk % ATTN_QW == 0
    assert qt.shape[2:] == (nblk, HEAD_PAD, blk) and vt.shape[2:] == (nblk, VT_ROWS, blk)
    chains = blk // ATTN_QW
    return pl.pallas_call(
        functools.partial(_attn_body, blk=blk),
        grid=(b, nh),
        in_specs=[
            pl.BlockSpec((None, None, nblk, HEAD_PAD, blk), lambda bi, h: (bi, h, 0, 0, 0)),
            pl.BlockSpec((None, None, s, HEAD_PAD), lambda bi, h: (bi, h, 0, 0)),
            pl.BlockSpec((None, None, nblk, VT_ROWS, blk), lambda bi, h: (bi, h, 0, 0, 0)),
        ],
        out_specs=pl.BlockSpec((None, s, V_HEAD), lambda bi, h: (bi, 0, h)),
        out_shape=jax.ShapeDtypeStruct((b, s, nh * V_HEAD), BF16),
        scratch_shapes=[pltpu.VMEM((chains, blk, ATTN_QW), F32),
                        pltpu.VMEM((chains, 1, ATTN_QW), F32),
                        pltpu.VMEM((chains, 1, ATTN_QW), F32),
                        pltpu.VMEM((chains, VT_ROWS, ATTN_QW), F32)],
        compiler_params=_params(40),
        name="mla_attention",
    )(qt, k, vt)


CONV_HALO = 32
CONV_PITCH = 4
CONV_ROWS = CONV_PITCH * SUBLANES
LN_ROWS = 256


def _conv_body(a_ref, prev_ref, w_ref, b_ref, g_ref, beta_ref, o_ref, buf_ref, y_ref):
    i = pl.program_id(1)
    tm, c = a_ref.shape
    tiles = c // LANES
    for lt in range(tiles):
        buf_ref[lt, CONV_HALO:, :] = a_ref[:, lt * LANES:(lt + 1) * LANES]

    @pl.when(i == 0)
    def _():
        buf_ref[:, :CONV_HALO, :] = jnp.zeros((tiles, CONV_HALO, LANES), F32)

    @pl.when(i > 0)
    def _():
        for lt in range(tiles):
            buf_ref[lt, :CONV_HALO, :] = prev_ref[:, lt * LANES:(lt + 1) * LANES]

    first = CONV_HALO - (CONV_WIDTH - 1)
    for lt in range(tiles):
        lanes = slice(lt * LANES, (lt + 1) * LANES)
        taps = [jnp.broadcast_to(w_ref[k:k + 1, lanes], (SUBLANES, LANES)) for k in range(CONV_WIDTH)]
        bias = jnp.broadcast_to(b_ref[:, lanes], (SUBLANES, LANES))

        def rows(r, carry, lt=lt, taps=taps, bias=bias):
            r0 = pl.multiple_of(r * CONV_ROWS, CONV_ROWS)
            acc = [[bias, None] for _ in range(CONV_PITCH)]
            for m in range(CONV_WIDTH + CONV_PITCH - 1):
                win = buf_ref[lt, pl.ds(r0 + first + m, SUBLANES, stride=CONV_PITCH), :]
                for j in range(CONV_PITCH):
                    if 0 <= m - j < CONV_WIDTH:
                        term = taps[m - j] * win
                        part = acc[j][m % 2]
                        acc[j][m % 2] = term if part is None else part + term
            for j in range(CONV_PITCH):
                y_ref[lt, pl.ds(r0 + j, SUBLANES, stride=CONV_PITCH), :] = acc[j][0] + acc[j][1]
            return carry

        lax.fori_loop(0, tm // CONV_ROWS, rows, 0)

    def norm(r, carry):
        r0 = pl.multiple_of(r * LN_ROWS, LN_ROWS)
        y = jnp.concatenate([y_ref[lt, pl.ds(r0, LN_ROWS), :] for lt in range(tiles)], axis=1)
        mu = jnp.mean(y, axis=-1, keepdims=True)
        yc = y - mu
        var = jnp.mean(yc * yc, axis=-1, keepdims=True)
        z = yc * lax.rsqrt(var + EPS) * g_ref[...] + beta_ref[...]
        o_ref[pl.ds(r0, LN_ROWS), :] = (z * jax.nn.sigmoid(z)).astype(BF16)
        return carry

    lax.fori_loop(0, tm // LN_ROWS, norm, 0)


def _conv(a, w, bias, g, beta, *, tm=512):
    b, s, c = a.shape
    tm = min(tm, s)
    assert tm % CONV_ROWS == 0 and tm % CONV_HALO == 0 and tm % LN_ROWS == 0 and c % LANES == 0
    per = tm // CONV_HALO
    tiles = c // LANES

    def const(shape):
        return pl.BlockSpec(shape, lambda bi, i: (0, 0))

    return pl.pallas_call(
        _conv_body,
        grid=(b, s // tm),
        in_specs=[
            pl.BlockSpec((None, tm, c), lambda bi, i: (bi, i, 0)),
            pl.BlockSpec((None, CONV_HALO, c), lambda bi, i: (bi, jnp.maximum(i * per - 1, 0), 0)),
            const(w.shape), const((1, c)), const((1, c)), const((1, c)),
        ],
        out_specs=pl.BlockSpec((None, tm, c), lambda bi, i: (bi, i, 0)),
        out_shape=jax.ShapeDtypeStruct((b, s, c), BF16),
        scratch_shapes=[pltpu.VMEM((tiles, CONV_HALO + tm, LANES), F32),
                        pltpu.VMEM((tiles, tm, LANES), F32)],
        compiler_params=_params(32),
        name="conv_module",
    )(a, a, w, bias, g, beta)


def _outproj_body(x_ref, yc_ref, ym_ref, wt_ref, wb_ref, o_ref):
    o_ref[...] = x_ref[...] + _dot(yc_ref[...], wt_ref[...]) + _dot(ym_ref[...], wb_ref[...])


def _outproj(x, yc, ym, w, layer, *, tm=512):
    b, s, d = x.shape
    tm = min(tm, s)
    cc = yc.shape[-1]
    assert w.shape[1] == 2 * cc

    def tile(c):
        return pl.BlockSpec((None, tm, c), lambda bi, i: (bi, i, 0))

    return pl.pallas_call(
        _outproj_body,
        grid=(b, s // tm),
        in_specs=[tile(d), tile(cc), tile(cc),
                  pl.BlockSpec((None, cc, d), lambda bi, i: (layer, 0, 0)),
                  pl.BlockSpec((None, cc, d), lambda bi, i: (layer, 1, 0))],
        out_specs=tile(d),
        out_shape=jax.ShapeDtypeStruct((b, s, d), F32),
        compiler_params=_params(48),
        name="outproj",
    )(x, yc, ym, w, w)


def _memkv_body(mem_ref, g_ref, wk_ref, wv_ref, gk_ref, k_ref, v_ref):
    m = _rms(mem_ref[...], g_ref[...]).astype(BF16)
    k_ref[...] = _rms(_dot(m, wk_ref[...].astype(BF16)), gk_ref[...]).astype(BF16)
    v_ref[...] = _dot(m, wv_ref[...].astype(BF16)).astype(BF16)


def _memkv(mem, g, wk, wv, layer, gk):
    rows, d = mem.shape
    hd = d // X_HEADS
    col = pl.BlockSpec((None, d, hd), lambda h: (layer, 0, h))
    out = pl.BlockSpec((rows, hd), lambda h: (0, h))
    shape = jax.ShapeDtypeStruct((rows, d), BF16)
    return pl.pallas_call(
        _memkv_body,
        grid=(X_HEADS,),
        in_specs=[pl.BlockSpec((rows, d), lambda h: (0, 0)), pl.BlockSpec((1, d), lambda h: (0, 0)),
                  col, col, pl.BlockSpec((1, hd), lambda h: (0, 0))],
        out_specs=[out, out],
        out_shape=[shape, shape],
        compiler_params=_params(40),
        name="mem_kv",
    )(mem, g, wk, wv, gk)


def _cross_body(x_ref, g_ref, wq_ref, gq_ref, k_ref, v_ref, wo_ref, o_ref, att_ref):
    x = x_ref[...]
    h = _rms(x, g_ref[...]).astype(BF16)
    hd = x.shape[-1] // X_HEADS
    scale = hd ** -0.5 * LOG2E
    cols = [slice(i * hd, (i + 1) * hd) for i in range(X_HEADS)]
    q = [_dot(h, wq_ref[:, c]) for c in cols]
    qn = [(qi * (lax.rsqrt(jnp.mean(qi * qi, axis=-1, keepdims=True) + EPS) * scale)
           * gq_ref[...]).astype(BF16) for qi in q]
    s = [_dot_nt(qi, k_ref[:, c]) for qi, c in zip(qn, cols)]
    p = [jnp.exp2(si - jnp.max(si, axis=-1, keepdims=True)) for si in s]
    for pi, c in zip(p, cols):
        l = jnp.sum(pi, axis=-1, keepdims=True)
        att_ref[:, c] = (_dot(pi.astype(BF16), v_ref[:, c]) / l).astype(BF16)
    o_ref[...] = x + _dot(att_ref[...], wo_ref[...])


def _cross(x, g, wq, gq, k, v, wo, layer, *, tm=512):
    b, s, d = x.shape
    tm = min(tm, s)
    m = k.shape[1]
    hd = d // X_HEADS

    def tile(c):
        return pl.BlockSpec((None, tm, c), lambda bi, i: (bi, i, 0))

    def const(shape):
        return pl.BlockSpec(shape, lambda bi, i: (0, 0))

    weight = pl.BlockSpec((None, d, d), lambda bi, i: (layer, 0, 0), pipeline_mode=pl.Buffered(1))

    mem = pl.BlockSpec((None, m, d), lambda bi, i: (bi, 0, 0))
    return pl.pallas_call(
        _cross_body,
        grid=(b, s // tm),
        in_specs=[tile(d), const((1, d)), weight, const((1, hd)), mem, mem, weight],
        out_specs=tile(d),
        out_shape=jax.ShapeDtypeStruct((b, s, d), F32),
        scratch_shapes=[pltpu.VMEM((tm, d), BF16)],
        compiler_params=_params(56),
        name="cross_attention",
    )(x, g, wq, gq, k, v, wo)


def _pad_cols(w, heads, width, padded):
    k = w.shape[0]
    w = w.reshape(k, heads, width)
    return jnp.pad(w, ((0, 0), (0, 0), (0, padded - width))).reshape(k, heads * padded)


def kernel(x, mem, positions, ffn1_norm, ffn1_w_gate, ffn1_w_up, ffn1_w_down, mix_norm, w_in, conv_w, conv_b, conv_ln_g, conv_ln_b, q_a_norm, w_q_b, kv_a_norm, w_kv_b, q_norm, k_norm, w_out, cross_norm, mem_norm, w_cq, w_ck, w_cv, cq_norm, ck_norm, w_co, ffn2_norm, ffn2_w_gate, ffn2_w_up, ffn2_w_down):
    b, s, d = x.shape
    n_mem = mem.shape[1]
    depth = w_in.shape[0]

    def row(v):
        return v.astype(F32)[None]

    tabs = _rope_tables(positions)
    mem2d = mem.reshape(b * n_mem, d)
    attn_blk = min(ATTN_BLOCK, s)
    ffn1 = (ffn1_w_gate, ffn1_w_up, ffn1_w_down)
    ffn2 = (ffn2_w_gate, ffn2_w_up, ffn2_w_down)
    w_out16, w_cq16, w_co16 = (_to_bf16(w) for w in (w_out, w_cq, w_co))
    w_in16 = w_in.astype(BF16)

    for l in range(depth):
        x = _ffn(x.reshape(b * s, d), row(ffn1_norm[l]), *ffn1, l).reshape(b, s, d)

        a, cq, ckv, kpe = _inproj(x, row(mix_norm[l]), w_in16, l, row(q_a_norm[l]), row(kv_a_norm[l]))

        wkv = w_kv_b[l].reshape(KV_LORA, MLA_HEADS, QK_NOPE + V_HEAD)
        wk = wkv[:, :, :QK_NOPE].reshape(KV_LORA, MLA_HEADS * QK_NOPE)
        wvt = wkv[:, :, QK_NOPE:].reshape(KV_LORA, MLA_HEADS * V_HEAD).T
        gq = jnp.broadcast_to(jnp.pad(q_norm[l], (0, HEAD_PAD - QK_HEAD))[:, None], (HEAD_PAD, LANES))
        gkp = jnp.pad(k_norm[l][QK_NOPE:], (0, LANES - QK_ROPE))
        q, k, vt = _qkv(cq, ckv, kpe, tabs,
                        _pad_cols(w_q_b[l], MLA_HEADS, QK_HEAD, HEAD_PAD).T.astype(BF16),
                        wk.astype(BF16), wvt.astype(BF16), gq.astype(F32), row(k_norm[l][:QK_NOPE]),
                        row(gkp), tm=attn_blk)
        y_mla = _attention(q, k, vt, blk=attn_blk)

        cw = jnp.pad(conv_w[l], ((0, CONV_HALO - CONV_WIDTH), (0, 0)))
        y_conv = _conv(a, cw, row(conv_b[l]), row(conv_ln_g[l]), row(conv_ln_b[l]))

        x = _outproj(x, y_conv, y_mla, w_out16, l)

        km, vm = _memkv(mem2d, row(mem_norm[l]), w_ck, w_cv, l, row(ck_norm[l]))
        x = _cross(x, row(cross_norm[l]), w_cq16, row(cq_norm[l]),
                   km.reshape(b, n_mem, d), vm.reshape(b, n_mem, d), w_co16, l)

        x = _ffn(x.reshape(b * s, d), row(ffn2_norm[l]), *ffn2, l).reshape(b, s, d)
    return x
```
